```python
import math
import jax, jax.numpy as jnp
from jax import lax
import numpy as np

D_MODEL = 2048
BATCH = 8
SEQ = 2048
DEPTH = 4
DEC_BATCH = 8
DEC_SEQ = 64
PAST_LEN = 2048

CHUNK = 64
Q_BLOCK = 128
GLA_BLOCK = CHUNK
GLA_HEADS = 4
GLA_DK = D_MODEL // 2
GLA_DV = D_MODEL
GLA_DK_HEAD = GLA_DK // GLA_HEADS
GLA_DV_HEAD = GLA_DV // GLA_HEADS
GLA_RANK = 16
GLA_TAU = 16.0
DIFF_HEADS = 8
DIFF_QK_HEAD = 64
DIFF_V_HEAD = 2 * DIFF_QK_HEAD
DIFF_QK = DIFF_HEADS * 2 * DIFF_QK_HEAD
DIFF_V = DIFF_HEADS * DIFF_V_HEAD
ROPE_DIM = DIFF_QK_HEAD // 4
ROPE_THETA = 500000.0
FFN_DIM = 5504
FFN_RES = 0.5
EPS = 1e-6
IN_SIZES = (GLA_DK, GLA_DK, GLA_DV, GLA_DV, GLA_RANK, DIFF_QK, DIFF_QK, DIFF_V, D_MODEL, D_MODEL)
IN_SPLITS = tuple(sum(IN_SIZES[:i + 1]) for i in range(len(IN_SIZES) - 1))
N_IN = sum(IN_SIZES)
F32 = jnp.float32

kernel_name = 'macaron_gla_diffattn_stream_step'


def rms_norm(x, gain):
    xf = x.astype(F32)
    y = xf * lax.rsqrt(jnp.mean(xf * xf, axis=-1, keepdims=True) + EPS)
    return (y * gain.astype(F32)).astype(x.dtype)


def half_ffn(x, gain, w_gate, w_up, w_down):
    h = rms_norm(x, gain)
    return x + FFN_RES * ((jax.nn.silu(h @ w_gate) * (h @ w_up)) @ w_down)


def rope_partial(x, pos):
    half = ROPE_DIM // 2
    inv = ROPE_THETA ** (-jnp.arange(half, dtype=F32) * 2.0 / ROPE_DIM)
    ang = pos.astype(F32)[:, None] * inv[None, :]
    cos = jnp.cos(ang)[None, :, None, None, :]
    sin = jnp.sin(ang)[None, :, None, None, :]
    xf = x.astype(F32)
    x1 = xf[..., :half]
    x2 = xf[..., half:ROPE_DIM]
    out = jnp.concatenate([x1 * cos - x2 * sin, x2 * cos + x1 * sin, xf[..., ROPE_DIM:]], axis=-1)
    return out.astype(x.dtype)


def mixer_inputs(h, pos, w_in, w_alpha2, b_alpha, q_norm, k_norm):
    B, L, _ = h.shape
    q_g, k_g, v_g, r_g, a_g, q_d, k_d, v_d, z_g, z_d = jnp.split(h @ w_in, IN_SPLITS, axis=-1)
    gla_q = q_g.reshape(B, L, GLA_HEADS, GLA_DK_HEAD) * (GLA_DK_HEAD ** -0.5)
    gla_k = k_g.reshape(B, L, GLA_HEADS, GLA_DK_HEAD)
    gla_v = v_g.reshape(B, L, GLA_HEADS, GLA_DV_HEAD)
    gla_g = (jax.nn.log_sigmoid((a_g @ w_alpha2 + b_alpha).astype(F32)) / GLA_TAU).reshape(B, L, GLA_HEADS, GLA_DK_HEAD)
    dq = rope_partial(rms_norm(q_d.reshape(B, L, DIFF_HEADS, 2, DIFF_QK_HEAD), q_norm), pos)
    dk = rope_partial(rms_norm(k_d.reshape(B, L, DIFF_HEADS, 2, DIFF_QK_HEAD), k_norm), pos)
    dv = v_d.reshape(B, L, DIFF_HEADS, DIFF_V_HEAD)
    return gla_q, gla_k, gla_v, gla_g, r_g, dq, dk, dv, z_g, z_d


def gla_block(q, k, v, g, s0):
    L = q.shape[1]
    qf, kf, vf = q.astype(F32), k.astype(F32), v.astype(F32)
    b = jnp.cumsum(g, axis=1)
    b_ref = b[:, L // 2:L // 2 + 1]
    b_end = b[:, L - 1:]
    o_inter = jnp.einsum('blhk,bhkv->blhv', qf * jnp.exp(b), s0)
    att = jnp.einsum('blhk,bshk->bhls', qf * jnp.exp(b - b_ref), kf * jnp.exp(b_ref - b))
    att = jnp.where(jnp.tril(jnp.ones((L, L), dtype=bool)), att, 0.0)
    o = o_inter + jnp.einsum('bhls,bshv->blhv', att, vf)
    s_new = jnp.exp(b_end[:, 0])[..., None] * s0 + jnp.einsum('blhk,blhv->bhkv', kf * jnp.exp(b_end - b), vf)
    return o.astype(q.dtype), s_new


def gla_prompt(q, k, v, g):
    B, S = q.shape[:2]
    nb = S // GLA_BLOCK

    def blocks(t):
        return t.reshape(B, nb, GLA_BLOCK, *t.shape[2:]).swapaxes(0, 1)

    def step(s, blk):
        o, s = gla_block(blk[0], blk[1], blk[2], blk[3], s)
        return s, o

    s0 = jnp.zeros((B, GLA_HEADS, GLA_DK_HEAD, GLA_DV_HEAD), F32)
    s_fin, o = lax.scan(step, s0, (blocks(q), blocks(k), blocks(v), blocks(g)))
    return o.swapaxes(0, 1).reshape(B, S, GLA_HEADS, GLA_DV_HEAD), s_fin


def lambda_value(lam_params, lambda_init):
    lp = lam_params.astype(F32)
    return jnp.exp(jnp.sum(lp[0] * lp[1])) - jnp.exp(jnp.sum(lp[2] * lp[3])) + lambda_init


def diff_attend(q, k, v, lam, mask):
    s = jnp.einsum('bqhmd,bkhmd->bhmqk', q, k).astype(F32) * (DIFF_QK_HEAD ** -0.5)
    if mask is not None:
        s = jnp.where(mask, s, -jnp.inf)
    p = jax.nn.softmax(s, axis=-1)
    a = p[:, :, 0] - lam * p[:, :, 1]
    return jnp.einsum('bhqk,bkhd->bqhd', a.astype(v.dtype), v)


def diff_prompt(q, k, v, lam):
    S = q.shape[1]
    outs = []
    for qb in range(S // Q_BLOCK):
        q0, q1 = qb * Q_BLOCK, (qb + 1) * Q_BLOCK
        qc = jnp.arange(q0, q1) // CHUNK
        kc = jnp.arange(q1) // CHUNK
        mask = qc[:, None] >= kc[None, :]
        outs.append(diff_attend(q[:, q0:q1], k[:, :q1], v[:, :q1], lam, mask))
    return jnp.concatenate(outs, axis=1)


def mixer_output(o_gla, r_g, o_diff, z_g, z_d, gla_norm, diff_norm, w_gla_o, w_diff_o, w_out, lambda_init):
    B, L = r_g.shape[:2]
    y_g = (rms_norm(o_gla, gla_norm).reshape(B, L, GLA_DV) * jax.nn.silu(r_g)) @ w_gla_o
    y_d = (rms_norm(o_diff, diff_norm) * (1.0 - lambda_init)).reshape(B, L, DIFF_V) @ w_diff_o
    return (jax.nn.sigmoid(z_g) * y_g + jax.nn.sigmoid(z_d) * y_d) @ w_out


def setup_inputs(seed: int = 0) -> dict:
    key = jax.random.key(seed)
    ks = jax.random.split(key, 32)

    def nrm(k, shape, scale):
        return jax.random.normal(k, shape, F32) * scale

    def gain(k, shape):
        return 1.0 + 0.02 * jax.random.normal(k, shape, F32)

    return {
        'x_prompt': nrm(ks[0], (BATCH, SEQ, D_MODEL), 1.0),
        'x_sample': nrm(ks[1], (DEC_BATCH, DEC_SEQ, D_MODEL), 1.0),
        'cache_diff_k': nrm(ks[2], (DEPTH, DEC_BATCH, PAST_LEN, DIFF_HEADS, 2 * DIFF_QK_HEAD), 1.0),
        'cache_diff_v': nrm(ks[3], (DEPTH, DEC_BATCH, PAST_LEN, DIFF_HEADS, DIFF_V_HEAD), 1.0),
        'state_gla': nrm(ks[4], (DEPTH, DEC_BATCH, GLA_HEADS, GLA_DK_HEAD, GLA_DV_HEAD), 1.0),
        'norm_ffn1': gain(ks[5], (DEPTH, D_MODEL)),
        'w_ffn1_gate': nrm(ks[6], (DEPTH, D_MODEL, FFN_DIM), D_MODEL ** -0.5),
        'w_ffn1_up': nrm(ks[7], (DEPTH, D_MODEL, FFN_DIM), D_MODEL ** -0.5),
        'w_ffn1_down': nrm(ks[8], (DEPTH, FFN_DIM, D_MODEL), FFN_DIM ** -0.5),
        'norm_mix': gain(ks[9], (DEPTH, D_MODEL)),
        'w_in': nrm(ks[10], (DEPTH, D_MODEL, N_IN), D_MODEL ** -0.5),
        'w_alpha2': nrm(ks[11], (DEPTH, GLA_RANK, GLA_DK), GLA_RANK ** -0.5),
        'b_alpha': nrm(ks[12], (DEPTH, GLA_DK), 0.1),
        'q_norm': gain(ks[13], (DEPTH, DIFF_QK_HEAD)),
        'k_norm': gain(ks[14], (DEPTH, DIFF_QK_HEAD)),
        'diff_lambda': nrm(ks[15], (DEPTH, 4, DIFF_QK_HEAD), 0.1),
        'gla_norm': gain(ks[16], (DEPTH, GLA_HEADS, GLA_DV_HEAD)),
        'diff_norm': gain(ks[17], (DEPTH, DIFF_HEADS, DIFF_V_HEAD)),
        'w_gla_o': nrm(ks[18], (DEPTH, GLA_DV, D_MODEL), GLA_DV ** -0.5),
        'w_diff_o': nrm(ks[19], (DEPTH, DIFF_V, D_MODEL), DIFF_V ** -0.5),
        'w_out': nrm(ks[20], (DEPTH, D_MODEL, D_MODEL), D_MODEL ** -0.5),
        'norm_ffn2': gain(ks[21], (DEPTH, D_MODEL)),
        'w_ffn2_gate': nrm(ks[22], (DEPTH, D_MODEL, FFN_DIM), D_MODEL ** -0.5),
        'w_ffn2_up': nrm(ks[23], (DEPTH, D_MODEL, FFN_DIM), D_MODEL ** -0.5),
        'w_ffn2_down': nrm(ks[24], (DEPTH, FFN_DIM, D_MODEL), FFN_DIM ** -0.5),
    }


def reference(x_prompt, x_sample, cache_diff_k, cache_diff_v, state_gla, norm_ffn1, w_ffn1_gate, w_ffn1_up,
              w_ffn1_down, norm_mix, w_in, w_alpha2, b_alpha, q_norm, k_norm, diff_lambda, gla_norm, diff_norm,
              w_gla_o, w_diff_o, w_out, norm_ffn2, w_ffn2_gate, w_ffn2_up, w_ffn2_down):
    B, S = x_prompt.shape[:2]
    DB, DS = x_sample.shape[:2]
    past = cache_diff_k.shape[2]
    pos_p = jnp.arange(S)
    pos_s = past + jnp.arange(DS)
    xp, xs = x_prompt, x_sample
    kp_list, vp_list, sp_list, ks_list, vs_list, ss_list = [], [], [], [], [], []
    for l in range(DEPTH):
        lambda_init = 0.8 - 0.6 * math.exp(-0.3 * l)
        lam = lambda_value(diff_lambda[l], lambda_init)
        xp = half_ffn(xp, norm_ffn1[l], w_ffn1_gate[l], w_ffn1_up[l], w_ffn1_down[l])
        xs = half_ffn(xs, norm_ffn1[l], w_ffn1_gate[l], w_ffn1_up[l], w_ffn1_down[l])
        gq, gk, gv, gg, r, dq, dk, dv, zg, zd = mixer_inputs(rms_norm(xp, norm_mix[l]), pos_p, w_in[l],
                                                            w_alpha2[l], b_alpha[l], q_norm[l], k_norm[l])
        o_g, s_fin = gla_prompt(gq, gk, gv, gg)
        o_d = diff_prompt(dq, dk, dv, lam)
        xp = xp + mixer_output(o_g, r, o_d, zg, zd, gla_norm[l], diff_norm[l], w_gla_o[l], w_diff_o[l],
                               w_out[l], lambda_init)
        kp_list.append(dk.reshape(B, S, DIFF_HEADS, 2 * DIFF_QK_HEAD))
        vp_list.append(dv)
        sp_list.append(s_fin.astype(xp.dtype))
        gq, gk, gv, gg, r, dq, dk, dv, zg, zd = mixer_inputs(rms_norm(xs, norm_mix[l]), pos_s, w_in[l],
                                                            w_alpha2[l], b_alpha[l], q_norm[l], k_norm[l])
        o_g, s_new = gla_block(gq, gk, gv, gg, state_gla[l].astype(F32))
        k_all = jnp.concatenate([cache_diff_k[l].reshape(DB, past, DIFF_HEADS, 2, DIFF_QK_HEAD).astype(dk.dtype), dk], axis=1)
        v_all = jnp.concatenate([cache_diff_v[l].astype(dv.dtype), dv], axis=1)
        o_d = diff_attend(dq, k_all, v_all, lam, None)
        xs = xs + mixer_output(o_g, r, o_d, zg, zd, gla_norm[l], diff_norm[l], w_gla_o[l], w_diff_o[l],
                               w_out[l], lambda_init)
        ks_list.append(dk.reshape(DB, DS, DIFF_HEADS, 2 * DIFF_QK_HEAD))
        vs_list.append(dv)
        ss_list.append(s_new.astype(xs.dtype))
        xp = half_ffn(xp, norm_ffn2[l], w_ffn2_gate[l], w_ffn2_up[l], w_ffn2_down[l])
        xs = half_ffn(xs, norm_ffn2[l], w_ffn2_gate[l], w_ffn2_up[l], w_ffn2_down[l])
    return (xp, xs, jnp.stack(kp_list), jnp.stack(vp_list), jnp.stack(sp_list),
            jnp.stack(ks_list), jnp.stack(vs_list), jnp.stack(ss_list))
```

```python
import functools
import math

import jax
import jax.numpy as jnp
from jax import lax
from jax.experimental import pallas as pl
from jax.experimental.pallas import tpu as pltpu

F32 = jnp.float32
BF16 = jnp.bfloat16

EPS = 1e-6
CHUNK = 64
GLA_HEADS = 4
GLA_DK_HEAD = 256
GLA_DV_HEAD = 512
GLA_DK = GLA_HEADS * GLA_DK_HEAD
GLA_DV = GLA_HEADS * GLA_DV_HEAD
GLA_RANK = 16
GLA_TAU = 16.0
DIFF_HEADS = 8
DIFF_QK_HEAD = 64
DIFF_V_HEAD = 128
DIFF_QK = DIFF_HEADS * 2 * DIFF_QK_HEAD
DIFF_V = DIFF_HEADS * DIFF_V_HEAD
ROPE_DIM = 16
ROPE_THETA = 500000.0
FFN_RES = 0.5

LANES = 128
VMEM_LIMIT_BYTES = 56 * 1024 * 1024

COL_VG = 0
COL_RG = COL_VG + GLA_DV
COL_ZG = COL_RG + GLA_DV
COL_ZD = COL_ZG + GLA_DV
COL_QG = COL_ZD + GLA_DV
COL_KG = COL_QG + GLA_DK
COL_QD = COL_KG + GLA_DK
COL_KD = COL_QD + DIFF_QK
COL_VD = COL_KD + DIFF_QK
P_COLS = COL_VD + DIFF_V
PROJ_TN = 1024


def _pick_tile(n, candidates):
    for c in candidates:
        if n % c == 0:
            return c
    raise ValueError(f"no tile in {candidates} divides {n}")


def _nt_dot(a, b):
    return lax.dot_general(a, b, (((1,), (1,)), ((), ())), preferred_element_type=F32)


def _tn_dot(a, b):
    return lax.dot_general(a, b, (((0,), (0,)), ((), ())), preferred_element_type=F32)


def _dot(a, b):
    return jnp.dot(a, b, preferred_element_type=F32)


def _rms(x):
    return x * lax.rsqrt(jnp.mean(x * x, axis=-1, keepdims=True) + EPS)


def _split2(x):
    hi = x.astype(BF16)
    lo = (x - hi.astype(F32)).astype(BF16)
    return hi, lo


def _ffn_body(x_ref, gain_ref, wgu_ref, wd_ref, o_ref, h_ref, *, tf):
    j = pl.program_id(1)
    last = pl.num_programs(1) - 1

    @pl.when(j == 0)
    def _():
        h_ref[...] = (_rms(x_ref[...]) * gain_ref[...]).astype(BF16)

    gu = _dot(h_ref[...], wgu_ref[...])
    g = gu[:, :tf]
    u = gu[:, tf:]
    a = (g * jax.nn.sigmoid(g) * u).astype(BF16)
    y = _dot(a, wd_ref[...])

    @pl.when(j == 0)
    def _():
        o_ref[...] = y

    @pl.when(j > 0)
    def _():
        o_ref[...] += y

    @pl.when(j == last)
    def _():
        o_ref[...] = x_ref[...] + FFN_RES * o_ref[...]


def _ffn(x, gain, wgu, wd, *, tf):
    t, d = x.shape
    nf = wd.shape[0] // tf
    tm = _pick_tile(t, (768, 512, 256, 128, 64))
    return pl.pallas_call(
        functools.partial(_ffn_body, tf=tf),
        grid=(t // tm, nf),
        in_specs=[
            pl.BlockSpec((tm, d), lambda i, j: (i, 0)),
            pl.BlockSpec((1, d), lambda i, j: (0, 0)),
            pl.BlockSpec((d, 2 * tf), lambda i, j: (0, j)),
            pl.BlockSpec((tf, d), lambda i, j: (j, 0)),
        ],
        out_specs=pl.BlockSpec((tm, d), lambda i, j: (i, 0)),
        out_shape=jax.ShapeDtypeStruct((t, d), F32),
        scratch_shapes=[pltpu.VMEM((tm, d), BF16)],
        compiler_params=pltpu.CompilerParams(
            dimension_semantics=("parallel", "arbitrary"), vmem_limit_bytes=VMEM_LIMIT_BYTES),
        name="ffn",
    )(x, gain, wgu, wd)


def _log_sigmoid(x):
    return jnp.minimum(x, 0.0) - jnp.log1p(jnp.exp(-jnp.abs(x)))


def _qk_norm_rope(x, gain, e_ref, et_ref, cos, sin_lo, sin_hi):
    reps = x.shape[1] // LANES
    s_hi, s_lo = _split2(x * x)
    ssum = _dot(s_hi, e_ref[...]) + _dot(s_lo, e_ref[...])
    r = lax.rsqrt(ssum * (1.0 / DIFF_QK_HEAD) + EPS)
    r_hi, r_lo = _split2(r)
    rb = _dot(r_hi, et_ref[...]) + _dot(r_lo, et_ref[...])
    y = x * rb * gain
    half = ROPE_DIM // 2
    tile = lambda t: jnp.concatenate([t] * reps, axis=1)
    return (y * tile(cos)
            + pltpu.roll(y, x.shape[1] - half, 1) * tile(sin_lo)
            + pltpu.roll(y, half, 1) * tile(sin_hi))


def _in_proj_body(x_ref, gain_ref, w_ref, wa_ref, w2_ref, ba_ref, qn_ref, kn_ref, e_ref, et_ref,
                  cos_ref, sl_ref, sh_ref, p_ref, g_ref, h_ref):
    j = pl.program_id(1)
    j_qg = COL_QG // PROJ_TN
    j_qd = COL_QD // PROJ_TN
    j_kd = COL_KD // PROJ_TN

    @pl.when(j == 0)
    def _():
        h = (_rms(x_ref[...]) * gain_ref[...]).astype(BF16)
        h_ref[...] = h
        a = _dot(h, wa_ref[...]).astype(BF16)
        z = _dot(a, w2_ref[...]) + ba_ref[...]
        g_ref[...] = _log_sigmoid(z) * (1.0 / GLA_TAU)

    acc = _dot(h_ref[...], w_ref[...])
    plain = (j != j_qg) & (j != j_qd) & (j != j_kd)

    @pl.when(plain)
    def _():
        p_ref[...] = acc

    @pl.when(j == j_qg)
    def _():
        p_ref[...] = acc * (GLA_DK_HEAD ** -0.5)

    @pl.when(j == j_qd)
    def _():
        p_ref[...] = _qk_norm_rope(acc, qn_ref[...], e_ref, et_ref, cos_ref[...], sl_ref[...], sh_ref[...])

    @pl.when(j == j_kd)
    def _():
        p_ref[...] = _qk_norm_rope(acc, kn_ref[...], e_ref, et_ref, cos_ref[...], sl_ref[...], sh_ref[...])


def _in_proj(x, gain, w, wa, w2, ba, qn, kn, e, et, cos, sin_lo, sin_hi):
    t, d = x.shape
    tm = _pick_tile(t, (768, 512, 256, 128, 64))
    tn = PROJ_TN
    const = lambda shape: pl.BlockSpec(shape, lambda i, j: (0, 0))
    rows = lambda width: pl.BlockSpec((tm, width), lambda i, j: (i, 0))
    return pl.pallas_call(
        _in_proj_body,
        grid=(t // tm, P_COLS // tn),
        in_specs=[
            rows(d), const((1, d)),
            pl.BlockSpec((d, tn), lambda i, j: (0, j)),
            const((d, LANES)), const((LANES, GLA_DK)), const((1, GLA_DK)),
            const((1, DIFF_QK)), const((1, DIFF_QK)),
            const((DIFF_QK, LANES)), const((LANES, DIFF_QK)),
            rows(LANES), rows(LANES), rows(LANES),
        ],
        out_specs=[pl.BlockSpec((tm, tn), lambda i, j: (i, j)), rows(GLA_DK)],
        out_shape=[jax.ShapeDtypeStruct((t, P_COLS), F32), jax.ShapeDtypeStruct((t, GLA_DK), F32)],
        scratch_shapes=[pltpu.VMEM((tm, d), BF16)],
        compiler_params=pltpu.CompilerParams(
            dimension_semantics=("parallel", "arbitrary"), vmem_limit_bytes=VMEM_LIMIT_BYTES),
        name="in_proj",
    )(x, gain, w, wa, w2, ba, qn, kn, e, et, cos, sin_lo, sin_hi)


def _gla_body(q_ref, k_ref, v_ref, r_ref, g_ref, s0_ref, gn_ref, o_ref, sout_ref, st_ref,
              *, n_prompt_chunks, chunks_per_seq):
    c = pl.program_id(0)
    is_prompt = c < n_prompt_chunks
    pos = c % chunks_per_seq
    is_last = jnp.logical_or(jnp.logical_not(is_prompt), pos == chunks_per_seq - 1)

    @pl.when(jnp.logical_and(is_prompt, pos == 0))
    def _():
        st_ref[...] = jnp.zeros_like(st_ref)

    @pl.when(jnp.logical_not(is_prompt))
    def _():
        for h in range(GLA_HEADS):
            st_ref[h] = s0_ref[0, h].T

    row = lax.broadcasted_iota(jnp.int32, (CHUNK, CHUNK), 0)
    col = lax.broadcasted_iota(jnp.int32, (CHUNK, CHUNK), 1)
    causal = row >= col
    tri = jnp.where(causal, 1.0, 0.0).astype(BF16)

    g = g_ref[...]
    g_hi = g.astype(BF16)
    g_r1 = g - g_hi.astype(F32)
    g_mid = g_r1.astype(BF16)
    g_lo = (g_r1 - g_mid.astype(F32)).astype(BF16)
    b = _dot(tri, g_hi) + _dot(tri, g_mid) + _dot(tri, g_lo)
    b_mid = b[CHUNK // 2:CHUNK // 2 + 1]
    b_end = b[CHUNK - 1:CHUNK]
    q = q_ref[...]
    k = k_ref[...]
    q_in = (q * jnp.exp(b)).astype(BF16)
    q_rel = (q * jnp.exp(b - b_mid)).astype(BF16)
    k_rel = (k * jnp.exp(b_mid - b)).astype(BF16)
    k_end = (k * jnp.exp(b_end - b)).astype(BF16)
    decay = jnp.exp(b_end)

    for h in range(GLA_HEADS):
        ks = slice(h * GLA_DK_HEAD, (h + 1) * GLA_DK_HEAD)
        vs = slice(h * GLA_DV_HEAD, (h + 1) * GLA_DV_HEAD)
        vb = v_ref[:, vs].astype(BF16)
        st = st_ref[h]
        o = _nt_dot(q_in[:, ks], st.astype(BF16))
        att = jnp.where(causal, _nt_dot(q_rel[:, ks], k_rel[:, ks]), 0.0)
        o = o + _dot(att.astype(BF16), vb)
        st_ref[h] = decay[:, ks] * st + _tn_dot(vb, k_end[:, ks])
        r = r_ref[:, vs]
        y = _rms(o) * gn_ref[h:h + 1, :] * (r * jax.nn.sigmoid(r))
        o_ref[:, vs] = y.astype(BF16)

    @pl.when(is_last)
    def _():
        for h in range(GLA_HEADS):
            sout_ref[0, h] = st_ref[h].T


def _gla(p, g, s0, gn, *, n_prompt_seqs, seq_len):
    t = p.shape[0]
    n_chunks = t // CHUNK
    chunks_per_seq = seq_len // CHUNK
    n_prompt_chunks = n_prompt_seqs * chunks_per_seq
    n_sample_seqs = s0.shape[0]
    n_seqs = n_prompt_seqs + n_sample_seqs

    def seq_of(c):
        return jnp.where(c < n_prompt_chunks, c // chunks_per_seq, n_prompt_seqs + c - n_prompt_chunks)

    state_block = (1, GLA_HEADS, GLA_DK_HEAD, GLA_DV_HEAD)
    return pl.pallas_call(
        functools.partial(_gla_body, n_prompt_chunks=n_prompt_chunks, chunks_per_seq=chunks_per_seq),
        grid=(n_chunks,),
        in_specs=[
            pl.BlockSpec((CHUNK, GLA_DK), lambda c: (c, COL_QG // GLA_DK)),
            pl.BlockSpec((CHUNK, GLA_DK), lambda c: (c, COL_KG // GLA_DK)),
            pl.BlockSpec((CHUNK, GLA_DV), lambda c: (c, COL_VG // GLA_DV)),
            pl.BlockSpec((CHUNK, GLA_DV), lambda c: (c, COL_RG // GLA_DV)),
            pl.BlockSpec((CHUNK, GLA_DK), lambda c: (c, 0)),
            pl.BlockSpec(state_block, lambda c: (jnp.maximum(c - n_prompt_chunks, 0), 0, 0, 0)),
            pl.BlockSpec((GLA_HEADS, GLA_DV_HEAD), lambda c: (0, 0)),
        ],
        out_specs=[
            pl.BlockSpec((CHUNK, GLA_DV), lambda c: (c, 0)),
            pl.BlockSpec(state_block, lambda c: (seq_of(c), 0, 0, 0)),
        ],
        out_shape=[
            jax.ShapeDtypeStruct((t, GLA_DV), BF16),
            jax.ShapeDtypeStruct((n_seqs,) + state_block[1:], F32),
        ],
        scratch_shapes=[pltpu.VMEM((GLA_HEADS, GLA_DV_HEAD, GLA_DK_HEAD), F32)],
        compiler_params=pltpu.CompilerParams(
            dimension_semantics=("arbitrary",), vmem_limit_bytes=VMEM_LIMIT_BYTES),
        name="gla",
    )(p, p, p, p, g, s0, gn)


def _lambda_value(lam_ref, lambda_init):
    lp = lam_ref[...]
    d1 = jnp.sum(lp[0:1] * lp[1:2], axis=-1, keepdims=True)
    d2 = jnp.sum(lp[2:3] * lp[3:4], axis=-1, keepdims=True)
    return jnp.exp(d1) - jnp.exp(d2) + lambda_init


def _split_maps(q):
    q = q * (DIFF_QK_HEAD ** -0.5)
    lane = lax.broadcasted_iota(jnp.int32, q.shape, 1)
    first = lane < DIFF_QK_HEAD
    return jnp.where(first, q, 0.0).astype(BF16), jnp.where(first, 0.0, q).astype(BF16)


def _attn_prompt_body(li_ref, q_ref, k_ref, v_ref, lam_ref, dn_ref, o_ref,
                      kb_ref, vb_ref, m_ref, l_ref, acc_ref, *, tq):
    i = pl.program_id(2)

    @pl.when(i == 0)
    def _():
        kb_ref[...] = k_ref[...].astype(BF16)
        vb_ref[...] = v_ref[...].astype(BF16)

    qz = _split_maps(q_ref[...])
    m_ref[...] = jnp.full_like(m_ref, -jnp.inf)
    l_ref[...] = jnp.zeros_like(l_ref)
    acc_ref[...] = jnp.zeros_like(acc_ref)

    def kv_step(j, mask):
        start = pl.multiple_of(j * tq, tq)
        kj = kb_ref[pl.ds(start, tq), :]
        vj = vb_ref[pl.ds(start, tq), :]
        for m in range(2):
            s = _nt_dot(qz[m], kj)
            if mask is not None:
                s = jnp.where(mask, s, -jnp.inf)
            m_old = m_ref[m]
            m_new = jnp.maximum(m_old, jnp.max(s, axis=-1, keepdims=True))
            alpha = jnp.exp(m_old - m_new)
            p = jnp.exp(s - m_new)
            l_ref[m] = alpha * l_ref[m] + jnp.sum(p, axis=-1, keepdims=True)
            acc_ref[m] = alpha * acc_ref[m] + _dot(p.astype(BF16), vj)
            m_ref[m] = m_new

    def body(j, carry):
        kv_step(j, None)
        return carry

    lax.fori_loop(0, i, body, 0)
    row = lax.broadcasted_iota(jnp.int32, (tq, tq), 0) // CHUNK
    col = lax.broadcasted_iota(jnp.int32, (tq, tq), 1) // CHUNK
    kv_step(i, row >= col)

    lambda_init = li_ref[0]
    lam = _lambda_value(lam_ref, lambda_init)
    o = acc_ref[0] / l_ref[0] - lam * (acc_ref[1] / l_ref[1])
    o_ref[...] = (_rms(o) * dn_ref[0] * (1.0 - lambda_init)).astype(BF16)


def _attn_prompt(p, li, lam, dn, *, n_seqs, seq_len):
    tq = _pick_tile(seq_len, (256, 128, 64))
    nq = seq_len // tq
    rows = n_seqs * seq_len
    dv = DIFF_V_HEAD
    return pl.pallas_call(
        functools.partial(_attn_prompt_body, tq=tq),
        grid=(n_seqs, DIFF_HEADS, nq),
        in_specs=[
            pl.BlockSpec(memory_space=pltpu.SMEM),
            pl.BlockSpec((tq, dv), lambda b, h, i: (b * nq + i, COL_QD // dv + h)),
            pl.BlockSpec((seq_len, dv), lambda b, h, i: (b, COL_KD // dv + h)),
            pl.BlockSpec((seq_len, dv), lambda b, h, i: (b, COL_VD // dv + h)),
            pl.BlockSpec((4, DIFF_QK_HEAD), lambda b, h, i: (0, 0)),
            pl.BlockSpec((1, 1, dv), lambda b, h, i: (h, 0, 0)),
        ],
        out_specs=pl.BlockSpec((tq, dv), lambda b, h, i: (b * nq + i, h)),
        out_shape=jax.ShapeDtypeStruct((rows, DIFF_V), BF16),
        scratch_shapes=[
            pltpu.VMEM((seq_len, dv), BF16), pltpu.VMEM((seq_len, dv), BF16),
            pltpu.VMEM((2, tq, 1), F32), pltpu.VMEM((2, tq, 1), F32), pltpu.VMEM((2, tq, dv), F32),
        ],
        compiler_params=pltpu.CompilerParams(
            dimension_semantics=("parallel", "parallel", "arbitrary"), vmem_limit_bytes=VMEM_LIMIT_BYTES),
        name="attn_prompt",
    )(li, p, p, p, lam, dn)


def _attn_sample_body(li_ref, q_ref, kc_ref, vc_ref, kn_ref, vn_ref, lam_ref, dn_ref, o_ref):
    qz = _split_maps(q_ref[...])
    kc = kc_ref[...].astype(BF16)
    kn = kn_ref[...].astype(BF16)
    probs = []
    for m in range(2):
        sc = _nt_dot(qz[m], kc)
        sn = _nt_dot(qz[m], kn)
        mx = jnp.maximum(jnp.max(sc, axis=-1, keepdims=True), jnp.max(sn, axis=-1, keepdims=True))
        pc = jnp.exp(sc - mx)
        pn = jnp.exp(sn - mx)
        denom = jnp.sum(pc, axis=-1, keepdims=True) + jnp.sum(pn, axis=-1, keepdims=True)
        probs.append((pc / denom, pn / denom))
    lambda_init = li_ref[0]
    lam = _lambda_value(lam_ref, lambda_init)
    ac = (probs[0][0] - lam * probs[1][0]).astype(BF16)
    an = (probs[0][1] - lam * probs[1][1]).astype(BF16)
    o = _dot(ac, vc_ref[...].astype(BF16)) + _dot(an, vn_ref[...].astype(BF16))
    o_ref[...] = (_rms(o) * dn_ref[0] * (1.0 - lambda_init)).astype(BF16)


def _attn_sample(p, cache_k, cache_v, li, lam, dn, *, n_seqs, seq_len, row_offset):
    past = cache_k.shape[0] // n_seqs
    dv = DIFF_V_HEAD
    blk0 = row_offset // seq_len
    new = lambda col: pl.BlockSpec((seq_len, dv), lambda b, h: (blk0 + b, col // dv + h))
    old = pl.BlockSpec((past, dv), lambda b, h: (b, h))
    return pl.pallas_call(
        _attn_sample_body,
        grid=(n_seqs, DIFF_HEADS),
        in_specs=[
            pl.BlockSpec(memory_space=pltpu.SMEM),
            new(COL_QD), old, old, new(COL_KD), new(COL_VD),
            pl.BlockSpec((4, DIFF_QK_HEAD), lambda b, h: (0, 0)),
            pl.BlockSpec((1, 1, dv), lambda b, h: (h, 0, 0)),
        ],
        out_specs=pl.BlockSpec((seq_len, dv), lambda b, h: (b, h)),
        out_shape=jax.ShapeDtypeStruct((n_seqs * seq_len, DIFF_V), BF16),
        compiler_params=pltpu.CompilerParams(
            dimension_semantics=("parallel", "parallel"), vmem_limit_bytes=VMEM_LIMIT_BYTES),
        name="attn_sample",
    )(li, p, cache_k, cache_v, p, p, lam, dn)


def _mix_out_body(x_ref, ag_ref, ad_ref, zg_ref, zd_ref, wgo_ref, wdo_ref, wo_ref, o_ref):
    yg = _dot(ag_ref[...], wgo_ref[...])
    yd = _dot(ad_ref[...], wdo_ref[...])
    m = jax.nn.sigmoid(zg_ref[...]) * yg + jax.nn.sigmoid(zd_ref[...]) * yd
    o_ref[...] = x_ref[...] + _dot(m.astype(BF16), wo_ref[...])


def _mix_out(x, ag, ad, p, wgo, wdo, wo):
    t, d = x.shape
    tm = _pick_tile(t, (256, 128, 64))
    rows = lambda width, col=0: pl.BlockSpec((tm, width), lambda i: (i, col // width))
    resident = lambda shape: pl.BlockSpec(shape, lambda i: (0, 0), pipeline_mode=pl.Buffered(1))
    return pl.pallas_call(
        _mix_out_body,
        grid=(t // tm,),
        in_specs=[
            rows(d), rows(GLA_DV), rows(DIFF_V), rows(d, COL_ZG), rows(d, COL_ZD),
            resident(wgo.shape), resident(wdo.shape), resident(wo.shape),
        ],
        out_specs=rows(d),
        out_shape=jax.ShapeDtypeStruct((t, d), F32),
        compiler_params=pltpu.CompilerParams(
            dimension_semantics=("parallel",), vmem_limit_bytes=VMEM_LIMIT_BYTES),
        name="mix_out",
    )(x, ag, ad, p, p, wgo, wdo, wo)


def _rope_tables(pos):
    half = ROPE_DIM // 2
    inv = ROPE_THETA ** (-jnp.arange(half, dtype=F32) * 2.0 / ROPE_DIM)
    ang = pos.astype(F32)[:, None] * inv[None, :]
    cos, sin = jnp.cos(ang), jnp.sin(ang)
    n = pos.shape[0]
    rest = DIFF_QK_HEAD - ROPE_DIM
    zeros = jnp.zeros((n, half), F32)
    pad = lambda parts, fill: jnp.tile(
        jnp.concatenate(parts + [jnp.full((n, rest), fill, F32)], axis=1), (1, LANES // DIFF_QK_HEAD))
    return pad([cos, cos], 1.0), pad([-sin, zeros], 0.0), pad([zeros, sin], 0.0)


def _ffn_weights(w_gate, w_up, w_down, tf):
    depth, d, f = w_gate.shape
    fp = -(-f // tf) * tf
    padc = lambda w: jnp.pad(w.astype(BF16), ((0, 0), (0, 0), (0, fp - f)))
    wgu = jnp.stack([padc(w_gate).reshape(depth, d, fp // tf, tf),
                     padc(w_up).reshape(depth, d, fp // tf, tf)], axis=3).reshape(depth, d, 2 * fp)
    wd = jnp.pad(w_down.astype(BF16), ((0, 0), (0, fp - f), (0, 0)))
    return wgu, wd


def kernel(x_prompt, x_sample, cache_diff_k, cache_diff_v, state_gla, norm_ffn1, w_ffn1_gate, w_ffn1_up,
           w_ffn1_down, norm_mix, w_in, w_alpha2, b_alpha, q_norm, k_norm, diff_lambda, gla_norm, diff_norm,
           w_gla_o, w_diff_o, w_out, norm_ffn2, w_ffn2_gate, w_ffn2_up, w_ffn2_down):
    nb, seq, d = x_prompt.shape
    ndb, dseq, _ = x_sample.shape
    depth = w_in.shape[0]
    past = cache_diff_k.shape[2]
    n_prompt = nb * seq
    n_sample = ndb * dseq
    assert seq % CHUNK == 0 and dseq == CHUNK and n_prompt % dseq == 0

    ffn_tf = 512
    wgu1, wd1 = _ffn_weights(w_ffn1_gate, w_ffn1_up, w_ffn1_down, ffn_tf)
    wgu2, wd2 = _ffn_weights(w_ffn2_gate, w_ffn2_up, w_ffn2_down, ffn_tf)
    sizes = (GLA_DK, GLA_DK, GLA_DV, GLA_DV, GLA_RANK, DIFF_QK, DIFF_QK, DIFF_V, d, d)
    offs = [0]
    for s in sizes:
        offs.append(offs[-1] + s)
    seg = lambda n: w_in[:, :, offs[n]:offs[n + 1]]
    w_main = jnp.concatenate([seg(2), seg(3), seg(8), seg(9), seg(0), seg(1), seg(5), seg(6), seg(7)],
                             axis=-1).astype(BF16)
    w_a = jnp.pad(seg(4), ((0, 0), (0, 0), (0, LANES - GLA_RANK))).astype(BF16)
    w_2 = jnp.pad(w_alpha2, ((0, 0), (0, LANES - GLA_RANK), (0, 0))).astype(BF16)
    n_groups = DIFF_QK // DIFF_QK_HEAD
    group = jnp.arange(DIFF_QK) // DIFF_QK_HEAD
    e = (group[:, None] == jnp.arange(LANES)[None, :]).astype(BF16)
    et = e.T
    qn = jnp.tile(q_norm, (1, n_groups)).reshape(depth, 1, DIFF_QK)
    kn = jnp.tile(k_norm, (1, n_groups)).reshape(depth, 1, DIFF_QK)
    pos = jnp.concatenate([jnp.tile(jnp.arange(seq), nb), jnp.tile(past + jnp.arange(dseq), ndb)])
    cos, sin_lo, sin_hi = _rope_tables(pos)
    wgo = w_gla_o.astype(BF16)
    wdo = w_diff_o.astype(BF16)
    wo = w_out.astype(BF16)

    x = jnp.concatenate([x_prompt.reshape(n_prompt, d), x_sample.reshape(n_sample, d)], axis=0)
    kp, vp, sp, ks, vs, ss = [], [], [], [], [], []
    for l in range(depth):
        lambda_init = 0.8 - 0.6 * math.exp(-0.3 * l)
        li = jnp.full((1,), lambda_init, F32)
        x = _ffn(x, norm_ffn1[l][None], wgu1[l], wd1[l], tf=ffn_tf)
        p, g = _in_proj(x, norm_mix[l][None], w_main[l], w_a[l], w_2[l], b_alpha[l][None], qn[l], kn[l],
                        e, et, cos, sin_lo, sin_hi)
        ag, states = _gla(p, g, state_gla[l], gla_norm[l], n_prompt_seqs=nb, seq_len=seq)
        dn = diff_norm[l].reshape(DIFF_HEADS, 1, DIFF_V_HEAD)
        ad_p = _attn_prompt(p, li, diff_lambda[l], dn, n_seqs=nb, seq_len=seq)
        ad_s = _attn_sample(p, cache_diff_k[l].reshape(ndb * past, DIFF_QK),
                            cache_diff_v[l].reshape(ndb * past, DIFF_V), li, diff_lambda[l], dn,
                            n_seqs=ndb, seq_len=dseq, row_offset=n_prompt)
        ad = jnp.concatenate([ad_p, ad_s], axis=0)
        x = _mix_out(x, ag, ad, p, wgo[l], wdo[l], wo[l])
        x = _ffn(x, norm_ffn2[l][None], wgu2[l], wd2[l], tf=ffn_tf)
        k_new = p[:, COL_KD:COL_KD + DIFF_QK]
        v_new = p[:, COL_VD:COL_VD + DIFF_V]
        kp.append(k_new[:n_prompt].reshape(nb, seq, DIFF_HEADS, 2 * DIFF_QK_HEAD))
        vp.append(v_new[:n_prompt].reshape(nb, seq, DIFF_HEADS, DIFF_V_HEAD))
        ks.append(k_new[n_prompt:].reshape(ndb, dseq, DIFF_HEADS, 2 * DIFF_QK_HEAD))
        vs.append(v_new[n_prompt:].reshape(ndb, dseq, DIFF_HEADS, DIFF_V_HEAD))
        sp.append(states[:nb])
        ss.append(states[nb:])
    return (x[:n_prompt].reshape(nb, seq, d), x[n_prompt:].reshape(ndb, dseq, d),
            jnp.stack(kp), jnp.stack(vp), jnp.stack(sp), jnp.stack(ks), jnp.stack(vs), jnp.stack(ss))
```

```python
import functools
import math

import jax
import jax.numpy as jnp
from jax import lax
from jax.experimental import pallas as pl
from jax.experimental.pallas import tpu as pltpu

F32 = jnp.float32
BF16 = jnp.bfloat16

EPS = 1e-6
CHUNK = 64
GLA_HEADS = 4
GLA_DK_HEAD = 256
GLA_DV_HEAD = 512
GLA_DK = GLA_HEADS * GLA_DK_HEAD
GLA_DV = GLA_HEADS * GLA_DV_HEAD
GLA_RANK = 16
GLA_TAU = 16.0
DIFF_HEADS = 8
DIFF_QK_HEAD = 64
DIFF_V_HEAD = 128
DIFF_QK = DIFF_HEADS * 2 * DIFF_QK_HEAD
DIFF_V = DIFF_HEADS * DIFF_V_HEAD
ROPE_DIM = 16
ROPE_THETA = 500000.0
FFN_RES = 0.5

LANES = 128
VMEM_LIMIT_BYTES = 56 * 1024 * 1024

W_QG = 0
W_KG = W_QG + GLA_DK
W_VG = W_KG + GLA_DK
W_RG = W_VG + GLA_DV
W_QD = W_RG + GLA_DV
W_KD = W_QD + DIFF_QK
W_VD = W_KD + DIFF_QK
W_ZG = W_VD + DIFF_V
P_QG = 0
P_KG = P_QG + GLA_DK
P_VG = P_KG + GLA_DK
P_RG = P_VG + GLA_DV
P_ZG = P_RG + GLA_DV
PROJ_TN = 1024


def _pick_tile(n, candidates):
    for c in candidates:
        if n % c == 0:
            return c
    raise ValueError(f"no tile in {candidates} divides {n}")


def _params(*semantics):
    return pltpu.CompilerParams(dimension_semantics=semantics, vmem_limit_bytes=VMEM_LIMIT_BYTES)


def _nt_dot(a, b):
    return lax.dot_general(a, b, (((1,), (1,)), ((), ())), preferred_element_type=F32)


def _tn_dot(a, b):
    return lax.dot_general(a, b, (((0,), (0,)), ((), ())), preferred_element_type=F32)


def _dot(a, b):
    return jnp.dot(a, b, preferred_element_type=F32)


def _rms(x):
    return x * lax.rsqrt(jnp.mean(x * x, axis=-1, keepdims=True) + EPS)


def _split2(x):
    hi = x.astype(BF16)
    lo = (x - hi.astype(F32)).astype(BF16)
    return hi, lo


def _stacked_out(prev, n_in):
    if prev is None:
        return [], [], {}
    return [prev], [pl.BlockSpec(memory_space=pl.ANY)], {n_in: 0}


def _ffn_body(*refs, tf, f_valid, dn, emit_h):
    if emit_h:
        x_ref, gain_ref, wg_ref, wu_ref, wd_ref, gain2_ref, o_ref, h2_ref, h_ref = refs
    else:
        x_ref, gain_ref, wg_ref, wu_ref, wd_ref, o_ref, h_ref = refs
    j = pl.program_id(1)
    last = pl.num_programs(1) - 1
    d = o_ref.shape[1]

    @pl.when(j == 0)
    def _():
        h_ref[...] = (_rms(x_ref[...]) * gain_ref[...]).astype(BF16)

    h = h_ref[...]
    g = _dot(h, wg_ref[...])
    u = _dot(h, wu_ref[...])
    a = (g * jax.nn.sigmoid(g) * u).astype(BF16)

    def accumulate(width, first):
        for n0 in range(0, d, dn):
            y = _dot(a[:, :width], wd_ref[:width, n0:n0 + dn])
            if first:
                o_ref[:, n0:n0 + dn] = y
            else:
                o_ref[:, n0:n0 + dn] += y

    @pl.when(j == 0)
    def _():
        accumulate(tf, True)

    @pl.when(jnp.logical_and(j > 0, j < last))
    def _():
        accumulate(tf, False)

    @pl.when(j == last)
    def _():
        accumulate(f_valid, False)
        xn = x_ref[...] + FFN_RES * o_ref[...]
        o_ref[...] = xn
        if emit_h:
            h2_ref[...] = (_rms(xn) * gain2_ref[...]).astype(BF16)


def _ffn(x, gain, wg, wu, wd, layer, gain_next=None):
    t, d = x.shape
    f = wg.shape[2]
    tm = _pick_tile(t, (1024, 512, 256, 128, 64))
    tf = 256
    nf = pl.cdiv(f, tf)
    f_valid = f - (nf - 1) * tf
    assert nf >= 2 and f_valid % LANES == 0
    emit_h = gain_next is not None
    rows = pl.BlockSpec((tm, d), lambda i, j: (i, 0))
    vec = pl.BlockSpec((1, d), lambda i, j: (0, 0))
    in_specs = [
        rows, vec,
        pl.BlockSpec((None, d, tf), lambda i, j: (layer, 0, j)),
        pl.BlockSpec((None, d, tf), lambda i, j: (layer, 0, j)),
        pl.BlockSpec((None, tf, d), lambda i, j: (layer, j, 0)),
    ]
    inputs = [x, gain, wg, wu, wd]
    out_specs = [rows]
    out_shape = [jax.ShapeDtypeStruct((t, d), F32)]
    if emit_h:
        in_specs.append(vec)
        inputs.append(gain_next)
        out_specs.append(rows)
        out_shape.append(jax.ShapeDtypeStruct((t, d), BF16))
    out = pl.pallas_call(
        functools.partial(_ffn_body, tf=tf, f_valid=f_valid, dn=512, emit_h=emit_h),
        grid=(t // tm, nf),
        in_specs=in_specs,
        out_specs=out_specs,
        out_shape=out_shape,
        scratch_shapes=[pltpu.VMEM((tm, d), BF16)],
        compiler_params=_params("parallel", "arbitrary"),
        name="ffn",
    )(*inputs)
    return out if emit_h else out[0]


def _qk_norm_rope(x, gain, e_ref, et_ref, cos, sin_lo, sin_hi):
    reps = x.shape[1] // LANES
    s_hi, s_lo = _split2(x * x)
    ssum = _dot(s_hi, e_ref[...]) + _dot(s_lo, e_ref[...])
    r = lax.rsqrt(ssum * (1.0 / DIFF_QK_HEAD) + EPS)
    r_hi, r_lo = _split2(r)
    rb = _dot(r_hi, et_ref[...]) + _dot(r_lo, et_ref[...])
    y = x * rb * gain
    half = ROPE_DIM // 2
    tile = lambda t: jnp.concatenate([t] * reps, axis=1)
    return (y * tile(cos)
            + pltpu.roll(y, x.shape[1] - half, 1) * tile(sin_lo)
            + pltpu.roll(y, half, 1) * tile(sin_hi))


def _proj_body(*refs, rope):
    if rope:
        h_ref, w_ref, gain_ref, e_ref, et_ref, cos_ref, sl_ref, sh_ref = refs[:8]
    else:
        h_ref, w_ref = refs[:2]
    o_ref = refs[-1]
    acc = _dot(h_ref[...], w_ref[...])
    if rope:
        acc = _qk_norm_rope(acc, gain_ref[...], e_ref, et_ref, cos_ref[...], sl_ref[...], sh_ref[...])
    o_ref[...] = acc.astype(o_ref.dtype)


def _proj(h, w, layer, col_tiles, out_dtype=F32, rope=None, stacked=None):
    t, d = h.shape
    tn = PROJ_TN
    tm = _pick_tile(t, (1024, 512, 256, 128, 64))
    nj = len(col_tiles)
    first = col_tiles[0]
    if nj > 1 and any(c != first + k for k, c in enumerate(col_tiles)):
        run = next(k for k, c in enumerate(col_tiles) if c != first + k)
        skip = col_tiles[run] - (first + run)
        assert all(c == first + k + (skip if k >= run else 0) for k, c in enumerate(col_tiles))
        tile_of = lambda j: first + j + jnp.where(j >= run, skip, 0)
    else:
        tile_of = lambda j: first + j
    in_specs = [
        pl.BlockSpec((tm, d), lambda j, i: (i, 0)),
        pl.BlockSpec((None, d, tn), lambda j, i: (layer, 0, tile_of(j))),
    ]
    inputs = [h, w]
    if rope is not None:
        gain, e, et, cos, sin_lo, sin_hi = rope
        assert nj == 1
        period = cos.shape[0] // tm
        const = lambda a: pl.BlockSpec(a.shape, lambda j, i: (0, 0))
        table = pl.BlockSpec((tm, LANES), lambda j, i: (i % period, 0))
        in_specs += [const(gain), const(e), const(et), table, table, table]
        inputs += [gain, e, et, cos, sin_lo, sin_hi]
    if stacked is None:
        out_spec = pl.BlockSpec((tm, tn), lambda j, i: (i, j))
        out_shape = jax.ShapeDtypeStruct((t, nj * tn), out_dtype)
        extra_in, extra_specs, aliases = [], [], {}
    else:
        depth, prev = stacked
        out_spec = pl.BlockSpec((None, tm, tn), lambda j, i: (layer, i, j))
        out_shape = jax.ShapeDtypeStruct((depth, t, nj * tn), out_dtype)
        extra_in, extra_specs, aliases = _stacked_out(prev, len(inputs))
    return pl.pallas_call(
        functools.partial(_proj_body, rope=rope is not None),
        grid=(nj, t // tm),
        in_specs=in_specs + extra_specs,
        out_specs=out_spec,
        out_shape=out_shape,
        input_output_aliases=aliases,
        compiler_params=_params("parallel", "parallel"),
        name="proj_rope" if rope is not None else "proj",
    )(*inputs, *extra_in)


def _log_sigmoid(x):
    return jnp.minimum(x, 0.0) - jnp.log1p(jnp.exp(-jnp.abs(x)))


def _gla_body(*refs, chunks_per_seq, has_s0, has_prev):
    refs = list(refs)
    q_ref, k_ref, v_ref, r_ref, h_ref, wa_ref, w2_ref, ba_ref, gn_ref = refs[:9]
    s0_ref = refs[9] if has_s0 else None
    o_ref, sout_ref, st_ref = refs[-3:]
    c = pl.program_id(0)
    pos = c % chunks_per_seq

    @pl.when(pos == 0)
    def _():
        if has_s0:
            for h in range(GLA_HEADS):
                st_ref[h] = s0_ref[h].T
        else:
            st_ref[...] = jnp.zeros_like(st_ref)

    row = lax.broadcasted_iota(jnp.int32, (CHUNK, CHUNK), 0)
    col = lax.broadcasted_iota(jnp.int32, (CHUNK, CHUNK), 1)
    causal = row >= col
    tri = jnp.where(causal, 1.0, 0.0).astype(BF16)

    a = _dot(h_ref[...], wa_ref[...]).astype(BF16)
    g = _log_sigmoid(_dot(a, w2_ref[...]) + ba_ref[...]) * (1.0 / GLA_TAU)
    g_hi = g.astype(BF16)
    g_r1 = g - g_hi.astype(F32)
    g_mid = g_r1.astype(BF16)
    g_lo = (g_r1 - g_mid.astype(F32)).astype(BF16)
    b = _dot(tri, g_hi) + _dot(tri, g_mid) + _dot(tri, g_lo)
    b_mid = b[CHUNK // 2:CHUNK // 2 + 1]
    b_end = b[CHUNK - 1:CHUNK]
    q = q_ref[...] * (GLA_DK_HEAD ** -0.5)
    k = k_ref[...]
    q_in = (q * jnp.exp(b)).astype(BF16)
    q_rel = (q * jnp.exp(b - b_mid)).astype(BF16)
    k_rel = (k * jnp.exp(b_mid - b)).astype(BF16)
    k_end = (k * jnp.exp(b_end - b)).astype(BF16)
    decay = jnp.exp(b_end)

    for h in range(GLA_HEADS):
        ks = slice(h * GLA_DK_HEAD, (h + 1) * GLA_DK_HEAD)
        vs = slice(h * GLA_DV_HEAD, (h + 1) * GLA_DV_HEAD)
        vb = v_ref[:, vs].astype(BF16)
        st = st_ref[h]
        o = _nt_dot(q_in[:, ks], st.astype(BF16))
        att = jnp.where(causal, _nt_dot(q_rel[:, ks], k_rel[:, ks]), 0.0)
        o = o + _dot(att.astype(BF16), vb)
        st_ref[h] = decay[:, ks] * st + _tn_dot(vb, k_end[:, ks])
        r = r_ref[:, vs]
        y = _rms(o) * gn_ref[h:h + 1, :] * (r * jax.nn.sigmoid(r))
        o_ref[:, vs] = y.astype(BF16)

    @pl.when(pos == chunks_per_seq - 1)
    def _():
        for h in range(GLA_HEADS):
            sout_ref[h] = st_ref[h].T


def _gla(p, h, wa, w2, ba, gn, layer, *, seq_len, states_prev, depth, s0=None):
    t, d = h.shape
    n_chunks = t // CHUNK
    cps = seq_len // CHUNK
    n_seqs = t // seq_len
    state_block = (None, None, GLA_HEADS, GLA_DK_HEAD, GLA_DV_HEAD)
    state_spec = pl.BlockSpec(state_block, lambda c: (layer, c // cps, 0, 0, 0))
    in_specs = [
        pl.BlockSpec((CHUNK, GLA_DK), lambda c: (c, P_QG // GLA_DK)),
        pl.BlockSpec((CHUNK, GLA_DK), lambda c: (c, P_KG // GLA_DK)),
        pl.BlockSpec((CHUNK, GLA_DV), lambda c: (c, P_VG // GLA_DV)),
        pl.BlockSpec((CHUNK, GLA_DV), lambda c: (c, P_RG // GLA_DV)),
        pl.BlockSpec((CHUNK, d), lambda c: (c, 0)),
        pl.BlockSpec((None, d, LANES), lambda c: (layer, 0, 0)),
        pl.BlockSpec((None, LANES, GLA_DK), lambda c: (layer, 0, 0)),
        pl.BlockSpec((None, 1, GLA_DK), lambda c: (layer, 0, 0)),
        pl.BlockSpec((None, GLA_HEADS, GLA_DV_HEAD), lambda c: (layer, 0, 0)),
    ]
    inputs = [p, p, p, p, h, wa, w2, ba, gn]
    if s0 is not None:
        in_specs.append(state_spec)
        inputs.append(s0)
    extra_in, extra_specs, aliases = _stacked_out(states_prev, len(inputs))
    aliases = {k: 1 for k in aliases}
    return pl.pallas_call(
        functools.partial(_gla_body, chunks_per_seq=cps, has_s0=s0 is not None,
                          has_prev=states_prev is not None),
        grid=(n_chunks,),
        in_specs=in_specs + extra_specs,
        out_specs=[pl.BlockSpec((CHUNK, GLA_DV), lambda c: (c, 0)), state_spec],
        out_shape=[
            jax.ShapeDtypeStruct((t, GLA_DV), BF16),
            jax.ShapeDtypeStruct((depth, n_seqs, GLA_HEADS, GLA_DK_HEAD, GLA_DV_HEAD), F32),
        ],
        input_output_aliases=aliases,
        scratch_shapes=[pltpu.VMEM((GLA_HEADS, GLA_DV_HEAD, GLA_DK_HEAD), F32)],
        compiler_params=_params("arbitrary"),
        name="gla",
    )(*inputs, *extra_in)


def _lambda_value(lam_ref, lambda_init):
    lp = lam_ref[...]
    d1 = jnp.sum(lp[0:1] * lp[1:2], axis=-1, keepdims=True)
    d2 = jnp.sum(lp[2:3] * lp[3:4], axis=-1, keepdims=True)
    return jnp.exp(d1) - jnp.exp(d2) + lambda_init


def _stack_maps(q):
    q = q.astype(F32) * (DIFF_QK_HEAD ** -0.5)
    lane = lax.broadcasted_iota(jnp.int32, q.shape, 1)
    first = lane < DIFF_QK_HEAD
    return jnp.concatenate([jnp.where(first, q, 0.0), jnp.where(first, 0.0, q)], axis=0).astype(BF16)


def _diff_weights(scores, lam):
    rows = scores[0].shape[0] // 2
    mx = functools.reduce(jnp.maximum, [jnp.max(s, axis=-1, keepdims=True) for s in scores])
    ps = [jnp.exp(s - mx) for s in scores]
    denom = functools.reduce(jnp.add, [jnp.sum(p, axis=-1, keepdims=True) for p in ps])
    inv = 1.0 / denom
    w0 = inv[:rows]
    w1 = lam * inv[rows:]
    return [(p[:rows] * w0 - p[rows:] * w1).astype(BF16) for p in ps]


def _attn_prompt_body(li_ref, q_ref, k_ref, v_ref, lam_ref, dn_ref, o_ref, *, tq):
    seq_len = q_ref.shape[0]
    kb = k_ref[...].astype(BF16)
    vb = v_ref[...].astype(BF16)
    lambda_init = li_ref[0]
    lam = _lambda_value(lam_ref, lambda_init)
    row = lax.broadcasted_iota(jnp.int32, (2 * tq, tq), 0) % tq // CHUNK
    col = lax.broadcasted_iota(jnp.int32, (2 * tq, tq), 1) // CHUNK
    visible = row >= col
    for i in range(seq_len // tq):
        lo, hi = i * tq, (i + 1) * tq
        qz = _stack_maps(q_ref[lo:hi, :])
        scores = [jnp.where(visible, _nt_dot(qz, kb[lo:hi]), -jnp.inf)]
        values = [vb[lo:hi]]
        if i > 0:
            scores.append(_nt_dot(qz, kb[:lo]))
            values.append(vb[:lo])
        weights = _diff_weights(scores, lam)
        o = functools.reduce(jnp.add, [_dot(w, v) for w, v in zip(weights, values)])
        o_ref[lo:hi, :] = (_rms(o) * dn_ref[...] * (1.0 - lambda_init)).astype(BF16)


def _attn_prompt(qd, k_all, v_all, li, lam, dn, layer, *, seq_len):
    t = qd.shape[0]
    n_seqs = t // seq_len
    tq = _pick_tile(seq_len, (256, 128, 64))
    dv = DIFF_V_HEAD
    kv = pl.BlockSpec((None, seq_len, dv), lambda b, h: (layer, b, h))
    return pl.pallas_call(
        functools.partial(_attn_prompt_body, tq=tq),
        grid=(n_seqs, DIFF_HEADS),
        in_specs=[
            pl.BlockSpec(memory_space=pltpu.SMEM),
            pl.BlockSpec((seq_len, dv), lambda b, h: (b, h)),
            kv, kv,
            pl.BlockSpec((None, 4, DIFF_QK_HEAD), lambda b, h: (layer, 0, 0)),
            pl.BlockSpec((None, None, 1, dv), lambda b, h: (layer, h, 0, 0)),
        ],
        out_specs=pl.BlockSpec((seq_len, dv), lambda b, h: (b, h)),
        out_shape=jax.ShapeDtypeStruct((t, DIFF_V), BF16),
        compiler_params=_params("parallel", "parallel"),
        name="attn_prompt",
    )(li, qd, k_all, v_all, lam, dn)


def _attn_sample_body(li_ref, q_ref, kc_ref, vc_ref, kn_ref, vn_ref, lam_ref, dn_ref, o_ref):
    qz = _stack_maps(q_ref[...])
    lambda_init = li_ref[0]
    lam = _lambda_value(lam_ref, lambda_init)
    scores = [_nt_dot(qz, kc_ref[...].astype(BF16)), _nt_dot(qz, kn_ref[...].astype(BF16))]
    wc, wn = _diff_weights(scores, lam)
    o = _dot(wc, vc_ref[...].astype(BF16)) + _dot(wn, vn_ref[...].astype(BF16))
    o_ref[...] = (_rms(o) * dn_ref[...] * (1.0 - lambda_init)).astype(BF16)


def _attn_sample(qd, k_all, v_all, cache_k, cache_v, li, lam, dn, layer, *, seq_len):
    t = qd.shape[0]
    n_seqs = t // seq_len
    past = cache_k.shape[1] // n_seqs
    dv = DIFF_V_HEAD
    new = pl.BlockSpec((None, seq_len, dv), lambda b, h: (layer, b, h))
    old = pl.BlockSpec((None, past, dv), lambda b, h: (layer, b, h))
    return pl.pallas_call(
        _attn_sample_body,
        grid=(n_seqs, DIFF_HEADS),
        in_specs=[
            pl.BlockSpec(memory_space=pltpu.SMEM),
            pl.BlockSpec((seq_len, dv), lambda b, h: (b, h)),
            old, old, new, new,
            pl.BlockSpec((None, 4, DIFF_QK_HEAD), lambda b, h: (layer, 0, 0)),
            pl.BlockSpec((None, None, 1, dv), lambda b, h: (layer, h, 0, 0)),
        ],
        out_specs=pl.BlockSpec((seq_len, dv), lambda b, h: (b, h)),
        out_shape=jax.ShapeDtypeStruct((t, DIFF_V), BF16),
        compiler_params=_params("parallel", "parallel"),
        name="attn_sample",
    )(li, qd, cache_k, cache_v, k_all, v_all, lam, dn)


def _mix_out_body(x_ref, ag_ref, ad_ref, zg_ref, zd_ref, wgo_ref, wdo_ref, wo_ref, o_ref):
    yg = _dot(ag_ref[...], wgo_ref[...])
    yd = _dot(ad_ref[...], wdo_ref[...])
    m = jax.nn.sigmoid(zg_ref[...]) * yg + jax.nn.sigmoid(zd_ref[...]) * yd
    o_ref[...] = x_ref[...] + _dot(m.astype(BF16), wo_ref[...])


def _mix_out(x, ag, ad, p, wgo, wdo, wo, layer):
    t, d = x.shape
    tm = _pick_tile(t, (256, 128, 64))
    rows = lambda width, col=0: pl.BlockSpec((tm, width), lambda i: (i, col // width))
    resident = lambda w: pl.BlockSpec((None,) + w.shape[1:], lambda i: (layer, 0, 0),
                                      pipeline_mode=pl.Buffered(1))
    return pl.pallas_call(
        _mix_out_body,
        grid=(t // tm,),
        in_specs=[
            rows(d), rows(GLA_DV), rows(DIFF_V), rows(d, P_ZG), rows(d, P_ZG + d),
            resident(wgo), resident(wdo), resident(wo),
        ],
        out_specs=rows(d),
        out_shape=jax.ShapeDtypeStruct((t, d), F32),
        compiler_params=_params("parallel"),
        name="mix_out",
    )(x, ag, ad, p, p, wgo, wdo, wo)


def _rope_tables(pos):
    half = ROPE_DIM // 2
    inv = ROPE_THETA ** (-jnp.arange(half, dtype=F32) * 2.0 / ROPE_DIM)
    ang = pos.astype(F32)[:, None] * inv[None, :]
    cos, sin = jnp.cos(ang), jnp.sin(ang)
    n = pos.shape[0]
    rest = DIFF_QK_HEAD - ROPE_DIM
    zeros = jnp.zeros((n, half), F32)
    pad = lambda parts, fill: jnp.tile(
        jnp.concatenate(parts + [jnp.full((n, rest), fill, F32)], axis=1), (1, LANES // DIFF_QK_HEAD))
    return pad([cos, cos], 1.0), pad([-sin, zeros], 0.0), pad([zeros, sin], 0.0)


def kernel(x_prompt, x_sample, cache_diff_k, cache_diff_v, state_gla, norm_ffn1, w_ffn1_gate, w_ffn1_up,
           w_ffn1_down, norm_mix, w_in, w_alpha2, b_alpha, q_norm, k_norm, diff_lambda, gla_norm, diff_norm,
           w_gla_o, w_diff_o, w_out, norm_ffn2, w_ffn2_gate, w_ffn2_up, w_ffn2_down):
    nb, seq, d = x_prompt.shape
    ndb, dseq, _ = x_sample.shape
    depth = w_in.shape[0]
    past = cache_diff_k.shape[2]
    assert seq % CHUNK == 0 and dseq == CHUNK and d == GLA_DV

    bf = lambda w: w.astype(BF16)
    wg1, wu1, wd1 = bf(w_ffn1_gate), bf(w_ffn1_up), bf(w_ffn1_down)
    wg2, wu2, wd2 = bf(w_ffn2_gate), bf(w_ffn2_up), bf(w_ffn2_down)
    a0 = W_QD
    w_main = jnp.concatenate([w_in[:, :, :a0], w_in[:, :, a0 + GLA_RANK:]], axis=-1).astype(BF16)
    w_a = jnp.pad(w_in[:, :, a0:a0 + GLA_RANK], ((0, 0), (0, 0), (0, LANES - GLA_RANK))).astype(BF16)
    w_2 = jnp.pad(w_alpha2, ((0, 0), (0, LANES - GLA_RANK), (0, 0))).astype(BF16)
    b_a = b_alpha.reshape(depth, 1, GLA_DK)
    wgo, wdo, wo = bf(w_gla_o), bf(w_diff_o), bf(w_out)
    n_groups = DIFF_QK // DIFF_QK_HEAD
    group = jnp.arange(DIFF_QK) // DIFF_QK_HEAD
    e = (group[:, None] == jnp.arange(LANES)[None, :]).astype(BF16)
    et = e.T
    qn = jnp.tile(q_norm, (1, n_groups)).reshape(depth, 1, DIFF_QK)
    kn = jnp.tile(k_norm, (1, n_groups)).reshape(depth, 1, DIFF_QK)
    dn = diff_norm.reshape(depth, DIFF_HEADS, 1, DIFF_V_HEAD)
    main_tiles = ([W_QG // PROJ_TN + k for k in range((W_QD - W_QG) // PROJ_TN)]
                  + [W_ZG // PROJ_TN + k for k in range(2 * d // PROJ_TN)])

    streams = []
    for x0, n_seqs, slen, start in ((x_prompt, nb, seq, 0), (x_sample, ndb, dseq, past)):
        rows = n_seqs * slen
        tm = _pick_tile(rows, (1024, 512, 256, 128, 64))
        period = max(slen, tm)
        assert period % slen == 0 and period % tm == 0
        tables = _rope_tables(start + jnp.arange(period) % slen)
        streams.append(dict(x=x0.reshape(rows, d), slen=slen, tables=tables, k=None, v=None, s=None))
    cache_k = cache_diff_k.reshape(depth, ndb * past, DIFF_QK)
    cache_v = cache_diff_v.reshape(depth, ndb * past, DIFF_V)

    for l in range(depth):
        lambda_init = 0.8 - 0.6 * math.exp(-0.3 * l)
        li = jnp.full((1,), lambda_init, F32)
        for si, st in enumerate(streams):
            slen = st["slen"]
            x, h = _ffn(st["x"], norm_ffn1[l][None], wg1, wu1, wd1, l, gain_next=norm_mix[l][None])
            p = _proj(h, w_main, l, main_tiles)
            qd = _proj(h, w_main, l, [W_QD // PROJ_TN], out_dtype=BF16, rope=(qn[l], e, et) + st["tables"])
            st["k"] = _proj(h, w_main, l, [W_KD // PROJ_TN], rope=(kn[l], e, et) + st["tables"],
                            stacked=(depth, st["k"]))
            st["v"] = _proj(h, w_main, l, [W_VD // PROJ_TN], stacked=(depth, st["v"]))
            ag, st["s"] = _gla(p, h, w_a, w_2, b_a, gla_norm, l, seq_len=slen, states_prev=st["s"],
                               depth=depth, s0=state_gla if si == 1 else None)
            if si == 0:
                ad = _attn_prompt(qd, st["k"], st["v"], li, diff_lambda, dn, l, seq_len=slen)
            else:
                ad = _attn_sample(qd, st["k"], st["v"], cache_k, cache_v, li, diff_lambda, dn, l, seq_len=slen)
            x = _mix_out(x, ag, ad, p, wgo, wdo, wo, l)
            st["x"] = _ffn(x, norm_ffn2[l][None], wg2, wu2, wd2, l)

    pr, sa = streams
    head_k = (DIFF_HEADS, 2 * DIFF_QK_HEAD)
    head_v = (DIFF_HEADS, DIFF_V_HEAD)
    return (pr["x"].reshape(nb, seq, d), sa["x"].reshape(ndb, dseq, d),
            pr["k"].reshape(depth, nb, seq, *head_k), pr["v"].reshape(depth, nb, seq, *head_v), pr["s"],
            sa["k"].reshape(depth, ndb, dseq, *head_k), sa["v"].reshape(depth, ndb, dseq, *head_v), sa["s"])
```

```python
import functools
import math

import jax
import jax.numpy as jnp
from jax import lax
from jax.experimental import pallas as pl
from jax.experimental.pallas import tpu as pltpu

F32 = jnp.float32
BF16 = jnp.bfloat16

EPS = 1e-6
CHUNK = 64
GLA_HEADS = 4
GLA_DK_HEAD = 256
GLA_DV_HEAD = 512
GLA_DK = GLA_HEADS * GLA_DK_HEAD
GLA_DV = GLA_HEADS * GLA_DV_HEAD
GLA_RANK = 16
GLA_TAU = 16.0
DIFF_HEADS = 8
DIFF_QK_HEAD = 64
DIFF_V_HEAD = 128
DIFF_QK = DIFF_HEADS * 2 * DIFF_QK_HEAD
DIFF_V = DIFF_HEADS * DIFF_V_HEAD
ROPE_DIM = 16
ROPE_THETA = 500000.0
FFN_RES = 0.5

LANES = 128
VMEM_LIMIT_BYTES = 56 * 1024 * 1024

W_QG = 0
W_KG = W_QG + GLA_DK
W_VG = W_KG + GLA_DK
W_RG = W_VG + GLA_DV
W_QD = W_RG + GLA_DV
W_KD = W_QD + DIFF_QK
W_VD = W_KD + DIFF_QK
W_ZG = W_VD + DIFF_V
P_QG = 0
P_KG = P_QG + GLA_DK
P_VG = P_KG + GLA_DK
P_RG = P_VG + GLA_DV
P_ZG = P_RG + GLA_DV
PROJ_TN = 1024


def _pick_tile(n, candidates):
    for c in candidates:
        if n % c == 0:
            return c
    raise ValueError(f"no tile in {candidates} divides {n}")


def _params(*semantics):
    return pltpu.CompilerParams(dimension_semantics=semantics, vmem_limit_bytes=VMEM_LIMIT_BYTES)


def _nt_dot(a, b):
    return lax.dot_general(a, b, (((1,), (1,)), ((), ())), preferred_element_type=F32)


def _tn_dot(a, b):
    return lax.dot_general(a, b, (((0,), (0,)), ((), ())), preferred_element_type=F32)


def _dot(a, b):
    return jnp.dot(a, b, preferred_element_type=F32)


def _rms(x):
    return x * lax.rsqrt(jnp.mean(x * x, axis=-1, keepdims=True) + EPS)


def _split2(x):
    hi = x.astype(BF16)
    lo = (x - hi.astype(F32)).astype(BF16)
    return hi, lo


def _stacked_out(prev, n_in):
    if prev is None:
        return [], [], {}
    return [prev], [pl.BlockSpec(memory_space=pl.ANY)], {n_in: 0}


def _ffn_body(*refs, tf, f_valid, dn, emit_h):
    if emit_h:
        x_ref, gain_ref, wg_ref, wu_ref, wd_ref, gain2_ref, o_ref, h2_ref, h_ref = refs
    else:
        x_ref, gain_ref, wg_ref, wu_ref, wd_ref, o_ref, h_ref = refs
    j = pl.program_id(1)
    last = pl.num_programs(1) - 1
    d = o_ref.shape[1]

    @pl.when(j == 0)
    def _():
        h_ref[...] = (_rms(x_ref[...]) * gain_ref[...]).astype(BF16)
        o_ref[...] = jnp.zeros_like(o_ref)

    def step(width):
        h = h_ref[...]
        g = _dot(h, wg_ref[:, :width])
        u = _dot(h, wu_ref[:, :width])
        a = (g * jax.nn.sigmoid(g) * u).astype(BF16)
        for n0 in range(0, d, dn):
            o_ref[:, n0:n0 + dn] += _dot(a, wd_ref[:width, n0:n0 + dn])

    @pl.when(j < last)
    def _():
        step(tf)

    @pl.when(j == last)
    def _():
        step(f_valid)
        xn = x_ref[...] + FFN_RES * o_ref[...]
        o_ref[...] = xn
        if emit_h:
            h2_ref[...] = (_rms(xn) * gain2_ref[...]).astype(BF16)


def _ffn(x, gain, wg, wu, wd, layer, gain_next=None):
    t, d = x.shape
    f = wg.shape[2]
    tm = _pick_tile(t, (1024, 512, 256, 128, 64))
    tf = 256
    nf = pl.cdiv(f, tf)
    f_valid = f - (nf - 1) * tf
    assert nf >= 2 and f_valid % LANES == 0
    emit_h = gain_next is not None
    rows = pl.BlockSpec((tm, d), lambda i, j: (i, 0))
    vec = pl.BlockSpec((1, d), lambda i, j: (0, 0))
    in_specs = [
        rows, vec,
        pl.BlockSpec((None, d, tf), lambda i, j: (layer, 0, j)),
        pl.BlockSpec((None, d, tf), lambda i, j: (layer, 0, j)),
        pl.BlockSpec((None, tf, d), lambda i, j: (layer, j, 0)),
    ]
    inputs = [x, gain, wg, wu, wd]
    out_specs = [rows]
    out_shape = [jax.ShapeDtypeStruct((t, d), F32)]
    if emit_h:
        in_specs.append(vec)
        inputs.append(gain_next)
        out_specs.append(rows)
        out_shape.append(jax.ShapeDtypeStruct((t, d), BF16))
    out = pl.pallas_call(
        functools.partial(_ffn_body, tf=tf, f_valid=f_valid, dn=512, emit_h=emit_h),
        grid=(t // tm, nf),
        in_specs=in_specs,
        out_specs=out_specs,
        out_shape=out_shape,
        scratch_shapes=[pltpu.VMEM((tm, d), BF16)],
        compiler_params=_params("parallel", "arbitrary"),
        name="ffn",
    )(*inputs)
    return out if emit_h else out[0]


def _qk_norm_rope(x, gain, e_ref, et_ref, cos, sin_lo, sin_hi):
    reps = x.shape[1] // LANES
    s_hi, s_lo = _split2(x * x)
    ssum = _dot(s_hi, e_ref[...]) + _dot(s_lo, e_ref[...])
    r = lax.rsqrt(ssum * (1.0 / DIFF_QK_HEAD) + EPS)
    r_hi, r_lo = _split2(r)
    rb = _dot(r_hi, et_ref[...]) + _dot(r_lo, et_ref[...])
    y = x * rb * gain
    half = ROPE_DIM // 2
    tile = lambda t: jnp.concatenate([t] * reps, axis=1)
    return (y * tile(cos)
            + pltpu.roll(y, x.shape[1] - half, 1) * tile(sin_lo)
            + pltpu.roll(y, half, 1) * tile(sin_hi))


def _proj_body(*refs, rope):
    if rope:
        h_ref, w_ref, gain_ref, e_ref, et_ref, cos_ref, sl_ref, sh_ref = refs[:8]
    else:
        h_ref, w_ref = refs[:2]
    o_ref = refs[-1]
    acc = _dot(h_ref[...], w_ref[...])
    if rope:
        acc = _qk_norm_rope(acc, gain_ref[...], e_ref, et_ref, cos_ref[...], sl_ref[...], sh_ref[...])
    o_ref[...] = acc.astype(o_ref.dtype)


def _proj(h, w, layer, col_tiles, out_dtype=F32, rope=None, stacked=None):
    t, d = h.shape
    tn = PROJ_TN
    tm = _pick_tile(t, (1024, 512, 256, 128, 64))
    nj = len(col_tiles)
    first = col_tiles[0]
    if nj > 1 and any(c != first + k for k, c in enumerate(col_tiles)):
        run = next(k for k, c in enumerate(col_tiles) if c != first + k)
        skip = col_tiles[run] - (first + run)
        assert all(c == first + k + (skip if k >= run else 0) for k, c in enumerate(col_tiles))
        tile_of = lambda j: first + j + jnp.where(j >= run, skip, 0)
    else:
        tile_of = lambda j: first + j
    in_specs = [
        pl.BlockSpec((tm, d), lambda j, i: (i, 0)),
        pl.BlockSpec((None, d, tn), lambda j, i: (layer, 0, tile_of(j))),
    ]
    inputs = [h, w]
    if rope is not None:
        gain, e, et, cos, sin_lo, sin_hi = rope
        assert nj == 1
        period = cos.shape[0] // tm
        const = lambda a: pl.BlockSpec(a.shape, lambda j, i: (0, 0))
        table = pl.BlockSpec((tm, LANES), lambda j, i: (i % period, 0))
        in_specs += [const(gain), const(e), const(et), table, table, table]
        inputs += [gain, e, et, cos, sin_lo, sin_hi]
    if stacked is None:
        out_spec = pl.BlockSpec((tm, tn), lambda j, i: (i, j))
        out_shape = jax.ShapeDtypeStruct((t, nj * tn), out_dtype)
        extra_in, extra_specs, aliases = [], [], {}
    else:
        depth, prev = stacked
        out_spec = pl.BlockSpec((None, tm, tn), lambda j, i: (layer, i, j))
        out_shape = jax.ShapeDtypeStruct((depth, t, nj * tn), out_dtype)
        extra_in, extra_specs, aliases = _stacked_out(prev, len(inputs))
    return pl.pallas_call(
        functools.partial(_proj_body, rope=rope is not None),
        grid=(nj, t // tm),
        in_specs=in_specs + extra_specs,
        out_specs=out_spec,
        out_shape=out_shape,
        input_output_aliases=aliases,
        compiler_params=_params("parallel", "parallel"),
        name="proj_rope" if rope is not None else "proj",
    )(*inputs, *extra_in)


def _log_sigmoid(x):
    return jnp.minimum(x, 0.0) - jnp.log1p(jnp.exp(-jnp.abs(x)))


def _gla_chunk(q, k, v_ref, r_ref, h, wa_ref, w2_ref, ba_ref, gn_ref, st_ref, o_ref, causal, tri):
    a = _dot(h, wa_ref[...]).astype(BF16)
    g = _log_sigmoid(_dot(a, w2_ref[...]) + ba_ref[...]) * (1.0 / GLA_TAU)
    g_hi = g.astype(BF16)
    g_r1 = g - g_hi.astype(F32)
    g_mid = g_r1.astype(BF16)
    g_lo = (g_r1 - g_mid.astype(F32)).astype(BF16)
    b = _dot(tri, g_hi) + _dot(tri, g_mid) + _dot(tri, g_lo)
    b_mid = b[CHUNK // 2:CHUNK // 2 + 1]
    b_end = b[CHUNK - 1:CHUNK]
    q = q * (GLA_DK_HEAD ** -0.5)
    q_in = (q * jnp.exp(b)).astype(BF16)
    q_rel = (q * jnp.exp(b - b_mid)).astype(BF16)
    k_rel = (k * jnp.exp(b_mid - b)).astype(BF16)
    k_end = (k * jnp.exp(b_end - b)).astype(BF16)
    decay = jnp.exp(b_end)

    for hd in range(GLA_HEADS):
        ks = slice(hd * GLA_DK_HEAD, (hd + 1) * GLA_DK_HEAD)
        vs = slice(hd * GLA_DV_HEAD, (hd + 1) * GLA_DV_HEAD)
        vb = v_ref[:, vs].astype(BF16)
        st = st_ref[hd]
        o = _nt_dot(q_in[:, ks], st.astype(BF16))
        att = jnp.where(causal, _nt_dot(q_rel[:, ks], k_rel[:, ks]), 0.0)
        o = o + _dot(att.astype(BF16), vb)
        st_ref[hd] = decay[:, ks] * st + _tn_dot(vb, k_end[:, ks])
        r = r_ref[:, vs]
        y = _rms(o) * gn_ref[hd:hd + 1, :] * (r * jax.nn.sigmoid(r))
        o_ref[:, vs] = y.astype(BF16)


def _gla_body(*refs, n_group, has_s0):
    q_ref, k_ref, v_ref, r_ref, h_ref, wa_ref, w2_ref, ba_ref, gn_ref = refs[:9]
    s0_ref = refs[9] if has_s0 else None
    o_ref, sout_ref, st_ref = refs[-3:]
    c = pl.program_id(1)

    @pl.when(c == 0)
    def _():
        if has_s0:
            for s in range(n_group):
                for hd in range(GLA_HEADS):
                    st_ref[s, hd] = s0_ref[s, hd].T
        else:
            st_ref[...] = jnp.zeros_like(st_ref)

    row = lax.broadcasted_iota(jnp.int32, (CHUNK, CHUNK), 0)
    col = lax.broadcasted_iota(jnp.int32, (CHUNK, CHUNK), 1)
    causal = row >= col
    tri = jnp.where(causal, 1.0, 0.0).astype(BF16)
    for s in range(n_group):
        _gla_chunk(q_ref[s], k_ref[s], v_ref.at[s], r_ref.at[s], h_ref[s], wa_ref, w2_ref, ba_ref, gn_ref,
                   st_ref.at[s], o_ref.at[s], causal, tri)

    @pl.when(c == pl.num_programs(1) - 1)
    def _():
        for s in range(n_group):
            for hd in range(GLA_HEADS):
                sout_ref[s, hd] = st_ref[s, hd].T


def _gla(p, h, wa, w2, ba, gn, layer, *, seq_len, states_prev, depth, s0=None):
    t, d = h.shape
    cps = seq_len // CHUNK
    n_seqs = t // seq_len
    n_group = _pick_tile(n_seqs, (4, 2, 1)) if s0 is None else _pick_tile(n_seqs, (2, 1))
    p3 = p.reshape(n_seqs, seq_len, p.shape[1])
    h3 = h.reshape(n_seqs, seq_len, d)
    state_block = (None, n_group, GLA_HEADS, GLA_DK_HEAD, GLA_DV_HEAD)
    state_spec = pl.BlockSpec(state_block, lambda g, c: (layer, g, 0, 0, 0))
    cols = lambda width, col: pl.BlockSpec((n_group, CHUNK, width), lambda g, c: (g, c, col // width))
    in_specs = [
        cols(GLA_DK, P_QG), cols(GLA_DK, P_KG), cols(GLA_DV, P_VG), cols(GLA_DV, P_RG), cols(d, 0),
        pl.BlockSpec((None, d, LANES), lambda g, c: (layer, 0, 0)),
        pl.BlockSpec((None, LANES, GLA_DK), lambda g, c: (layer, 0, 0)),
        pl.BlockSpec((None, 1, GLA_DK), lambda g, c: (layer, 0, 0)),
        pl.BlockSpec((None, GLA_HEADS, GLA_DV_HEAD), lambda g, c: (layer, 0, 0)),
    ]
    inputs = [p3, p3, p3, p3, h3, wa, w2, ba, gn]
    if s0 is not None:
        in_specs.append(state_spec)
        inputs.append(s0)
    extra_in, extra_specs, aliases = _stacked_out(states_prev, len(inputs))
    aliases = {k: 1 for k in aliases}
    out, states = pl.pallas_call(
        functools.partial(_gla_body, n_group=n_group, has_s0=s0 is not None),
        grid=(n_seqs // n_group, cps),
        in_specs=in_specs + extra_specs,
        out_specs=[cols(GLA_DV, 0), state_spec],
        out_shape=[
            jax.ShapeDtypeStruct((n_seqs, seq_len, GLA_DV), BF16),
            jax.ShapeDtypeStruct((depth, n_seqs, GLA_HEADS, GLA_DK_HEAD, GLA_DV_HEAD), F32),
        ],
        input_output_aliases=aliases,
        scratch_shapes=[pltpu.VMEM((n_group, GLA_HEADS, GLA_DV_HEAD, GLA_DK_HEAD), F32)],
        compiler_params=_params("parallel", "arbitrary"),
        name="gla",
    )(*inputs, *extra_in)
    return out.reshape(t, GLA_DV), states


def _lambda_value(lam_ref, lambda_init):
    lp = lam_ref[...]
    d1 = jnp.sum(lp[0:1] * lp[1:2], axis=-1, keepdims=True)
    d2 = jnp.sum(lp[2:3] * lp[3:4], axis=-1, keepdims=True)
    return jnp.exp(d1) - jnp.exp(d2) + lambda_init


def _stack_maps(q):
    q = q.astype(F32) * (DIFF_QK_HEAD ** -0.5)
    lane = lax.broadcasted_iota(jnp.int32, q.shape, 1)
    first = lane < DIFF_QK_HEAD
    return jnp.concatenate([jnp.where(first, q, 0.0), jnp.where(first, 0.0, q)], axis=0).astype(BF16)


def _diff_weights(scores, lam):
    rows = scores[0].shape[0] // 2
    mx = functools.reduce(jnp.maximum, [jnp.max(s, axis=-1, keepdims=True) for s in scores])
    ps = [jnp.exp(s - mx) for s in scores]
    denom = functools.reduce(jnp.add, [jnp.sum(p, axis=-1, keepdims=True) for p in ps])
    inv = 1.0 / denom
    w0 = inv[:rows]
    w1 = lam * inv[rows:]
    return [(p[:rows] * w0 - p[rows:] * w1).astype(BF16) for p in ps]


def _attn_prompt_body(li_ref, q_ref, k_ref, v_ref, lam_ref, dn_ref, o_ref, *, tq):
    seq_len = q_ref.shape[0]
    kb = k_ref[...].astype(BF16)
    vb = v_ref[...].astype(BF16)
    lambda_init = li_ref[0]
    lam = _lambda_value(lam_ref, lambda_init)
    row = lax.broadcasted_iota(jnp.int32, (2 * tq, tq), 0) % tq // CHUNK
    col = lax.broadcasted_iota(jnp.int32, (2 * tq, tq), 1) // CHUNK
    visible = row >= col
    for i in range(seq_len // tq):
        lo, hi = i * tq, (i + 1) * tq
        qz = _stack_maps(q_ref[lo:hi, :])
        scores = [jnp.where(visible, _nt_dot(qz, kb[lo:hi]), -jnp.inf)]
        values = [vb[lo:hi]]
        if i > 0:
            scores.append(_nt_dot(qz, kb[:lo]))
            values.append(vb[:lo])
        weights = _diff_weights(scores, lam)
        o = functools.reduce(jnp.add, [_dot(w, v) for w, v in zip(weights, values)])
        o_ref[lo:hi, :] = (_rms(o) * dn_ref[...] * (1.0 - lambda_init)).astype(BF16)


def _attn_prompt(qd, k_all, v_all, li, lam, dn, layer, *, seq_len):
    t = qd.shape[0]
    n_seqs = t // seq_len
    tq = _pick_tile(seq_len, (256, 128, 64))
    dv = DIFF_V_HEAD
    kv = pl.BlockSpec((None, seq_len, dv), lambda b, h: (layer, b, h))
    return pl.pallas_call(
        functools.partial(_attn_prompt_body, tq=tq),
        grid=(n_seqs, DIFF_HEADS),
        in_specs=[
            pl.BlockSpec(memory_space=pltpu.SMEM),
            pl.BlockSpec((seq_len, dv), lambda b, h: (b, h)),
            kv, kv,
            pl.BlockSpec((None, 4, DIFF_QK_HEAD), lambda b, h: (layer, 0, 0)),
            pl.BlockSpec((None, None, 1, dv), lambda b, h: (layer, h, 0, 0)),
        ],
        out_specs=pl.BlockSpec((seq_len, dv), lambda b, h: (b, h)),
        out_shape=jax.ShapeDtypeStruct((t, DIFF_V), BF16),
        compiler_params=_params("parallel", "parallel"),
        name="attn_prompt",
    )(li, qd, k_all, v_all, lam, dn)


def _attn_sample_body(li_ref, q_ref, kc_ref, vc_ref, kn_ref, vn_ref, lam_ref, dn_ref, o_ref):
    past = kc_ref.shape[0] // DIFF_HEADS
    lambda_init = li_ref[0]
    lam = _lambda_value(lam_ref, lambda_init)
    for h in range(DIFF_HEADS):
        cols = slice(h * DIFF_V_HEAD, (h + 1) * DIFF_V_HEAD)
        head_rows = pl.ds(h, past, stride=DIFF_HEADS)
        qz = _stack_maps(q_ref[:, cols])
        scores = [_nt_dot(qz, kc_ref[head_rows, :].astype(BF16)), _nt_dot(qz, kn_ref[:, cols].astype(BF16))]
        wc, wn = _diff_weights(scores, lam)
        o = _dot(wc, vc_ref[head_rows, :].astype(BF16)) + _dot(wn, vn_ref[:, cols].astype(BF16))
        o_ref[:, cols] = (_rms(o) * dn_ref[h] * (1.0 - lambda_init)).astype(BF16)


def _attn_sample(qd, k_all, v_all, cache_k, cache_v, li, lam, dn, layer, *, seq_len):
    t = qd.shape[0]
    n_seqs = t // seq_len
    depth, _, past, heads, dv = cache_k.shape
    rows = past * heads
    new = pl.BlockSpec((None, seq_len, heads * dv), lambda b: (layer, b, 0))
    old = pl.BlockSpec((None, None, rows, dv), lambda b: (layer, b, 0, 0))
    return pl.pallas_call(
        _attn_sample_body,
        grid=(n_seqs,),
        in_specs=[
            pl.BlockSpec(memory_space=pltpu.SMEM),
            pl.BlockSpec((seq_len, heads * dv), lambda b: (b, 0)),
            old, old, new, new,
            pl.BlockSpec((None, 4, DIFF_QK_HEAD), lambda b: (layer, 0, 0)),
            pl.BlockSpec((None, heads, 1, dv), lambda b: (layer, 0, 0, 0)),
        ],
        out_specs=pl.BlockSpec((seq_len, heads * dv), lambda b: (b, 0)),
        out_shape=jax.ShapeDtypeStruct((t, DIFF_V), BF16),
        compiler_params=_params("parallel"),
        name="attn_sample",
    )(li, qd, cache_k.reshape(depth, n_seqs, rows, dv), cache_v.reshape(depth, n_seqs, rows, dv),
      k_all, v_all, lam, dn)


def _mix_out_body(x_ref, ag_ref, ad_ref, zg_ref, zd_ref, wgo_ref, wdo_ref, wo_ref, o_ref):
    yg = _dot(ag_ref[...], wgo_ref[...])
    yd = _dot(ad_ref[...], wdo_ref[...])
    m = jax.nn.sigmoid(zg_ref[...]) * yg + jax.nn.sigmoid(zd_ref[...]) * yd
    o_ref[...] = x_ref[...] + _dot(m.astype(BF16), wo_ref[...])


def _mix_out(x, ag, ad, p, wgo, wdo, wo, layer):
    t, d = x.shape
    tm = _pick_tile(t, (256, 128, 64))
    rows = lambda width, col=0: pl.BlockSpec((tm, width), lambda i: (i, col // width))
    resident = lambda w: pl.BlockSpec((None,) + w.shape[1:], lambda i: (layer, 0, 0),
                                      pipeline_mode=pl.Buffered(1))
    return pl.pallas_call(
        _mix_out_body,
        grid=(t // tm,),
        in_specs=[
            rows(d), rows(GLA_DV), rows(DIFF_V), rows(d, P_ZG), rows(d, P_ZG + d),
            resident(wgo), resident(wdo), resident(wo),
        ],
        out_specs=rows(d),
        out_shape=jax.ShapeDtypeStruct((t, d), F32),
        compiler_params=_params("parallel"),
        name="mix_out",
    )(x, ag, ad, p, p, wgo, wdo, wo)


def _rope_tables(pos):
    half = ROPE_DIM // 2
    inv = ROPE_THETA ** (-jnp.arange(half, dtype=F32) * 2.0 / ROPE_DIM)
    ang = pos.astype(F32)[:, None] * inv[None, :]
    cos, sin = jnp.cos(ang), jnp.sin(ang)
    n = pos.shape[0]
    rest = DIFF_QK_HEAD - ROPE_DIM
    zeros = jnp.zeros((n, half), F32)
    pad = lambda parts, fill: jnp.tile(
        jnp.concatenate(parts + [jnp.full((n, rest), fill, F32)], axis=1), (1, LANES // DIFF_QK_HEAD))
    return pad([cos, cos], 1.0), pad([-sin, zeros], 0.0), pad([zeros, sin], 0.0)


def kernel(x_prompt, x_sample, cache_diff_k, cache_diff_v, state_gla, norm_ffn1, w_ffn1_gate, w_ffn1_up,
           w_ffn1_down, norm_mix, w_in, w_alpha2, b_alpha, q_norm, k_norm, diff_lambda, gla_norm, diff_norm,
           w_gla_o, w_diff_o, w_out, norm_ffn2, w_ffn2_gate, w_ffn2_up, w_ffn2_down):
    nb, seq, d = x_prompt.shape
    ndb, dseq, _ = x_sample.shape
    depth = w_in.shape[0]
    past = cache_diff_k.shape[2]
    assert seq % CHUNK == 0 and dseq == CHUNK and d == GLA_DV

    bf = lambda w: w.astype(BF16)
    wg1, wu1, wd1 = bf(w_ffn1_gate), bf(w_ffn1_up), bf(w_ffn1_down)
    wg2, wu2, wd2 = bf(w_ffn2_gate), bf(w_ffn2_up), bf(w_ffn2_down)
    a0 = W_QD
    w_main = jnp.concatenate([w_in[:, :, :a0], w_in[:, :, a0 + GLA_RANK:]], axis=-1).astype(BF16)
    w_a = jnp.pad(w_in[:, :, a0:a0 + GLA_RANK], ((0, 0), (0, 0), (0, LANES - GLA_RANK))).astype(BF16)
    w_2 = jnp.pad(w_alpha2, ((0, 0), (0, LANES - GLA_RANK), (0, 0))).astype(BF16)
    b_a = b_alpha.reshape(depth, 1, GLA_DK)
    wgo, wdo, wo = bf(w_gla_o), bf(w_diff_o), bf(w_out)
    n_groups = DIFF_QK // DIFF_QK_HEAD
    group = jnp.arange(DIFF_QK) // DIFF_QK_HEAD
    e = (group[:, None] == jnp.arange(LANES)[None, :]).astype(BF16)
    et = e.T
    qn = jnp.tile(q_norm, (1, n_groups)).reshape(depth, 1, DIFF_QK)
    kn = jnp.tile(k_norm, (1, n_groups)).reshape(depth, 1, DIFF_QK)
    dn = diff_norm.reshape(depth, DIFF_HEADS, 1, DIFF_V_HEAD)
    main_tiles = ([W_QG // PROJ_TN + k for k in range((W_QD - W_QG) // PROJ_TN)]
                  + [W_ZG // PROJ_TN + k for k in range(2 * d // PROJ_TN)])

    streams = []
    for x0, n_seqs, slen, start in ((x_prompt, nb, seq, 0), (x_sample, ndb, dseq, past)):
        rows = n_seqs * slen
        tm = _pick_tile(rows, (1024, 512, 256, 128, 64))
        period = max(slen, tm)
        assert period % slen == 0 and period % tm == 0
        tables = _rope_tables(start + jnp.arange(period) % slen)
        streams.append(dict(x=x0.reshape(rows, d), slen=slen, tables=tables, k=None, v=None, s=None))

    for l in range(depth):
        lambda_init = 0.8 - 0.6 * math.exp(-0.3 * l)
        li = jnp.full((1,), lambda_init, F32)
        for si, st in enumerate(streams):
            slen = st["slen"]
            x, h = _ffn(st["x"], norm_ffn1[l][None], wg1, wu1, wd1, l, gain_next=norm_mix[l][None])
            p = _proj(h, w_main, l, main_tiles)
            qd = _proj(h, w_main, l, [W_QD // PROJ_TN], out_dtype=BF16, rope=(qn[l], e, et) + st["tables"])
            st["k"] = _proj(h, w_main, l, [W_KD // PROJ_TN], rope=(kn[l], e, et) + st["tables"],
                            stacked=(depth, st["k"]))
            st["v"] = _proj(h, w_main, l, [W_VD // PROJ_TN], stacked=(depth, st["v"]))
            ag, st["s"] = _gla(p, h, w_a, w_2, b_a, gla_norm, l, seq_len=slen, states_prev=st["s"],
                               depth=depth, s0=state_gla if si == 1 else None)
            if si == 0:
                ad = _attn_prompt(qd, st["k"], st["v"], li, diff_lambda, dn, l, seq_len=slen)
            else:
                ad = _attn_sample(qd, st["k"], st["v"], cache_diff_k, cache_diff_v, li, diff_lambda, dn, l,
                                  seq_len=slen)
            x = _mix_out(x, ag, ad, p, wgo, wdo, wo, l)
            st["x"] = _ffn(x, norm_ffn2[l][None], wg2, wu2, wd2, l)

    pr, sa = streams
    head_k = (DIFF_HEADS, 2 * DIFF_QK_HEAD)
    head_v = (DIFF_HEADS, DIFF_V_HEAD)
    return (pr["x"].reshape(nb, seq, d), sa["x"].reshape(ndb, dseq, d),
            pr["k"].reshape(depth, nb, seq, *head_k), pr["v"].reshape(depth, nb, seq, *head_v), pr["s"],
            sa["k"].reshape(depth, ndb, dseq, *head_k), sa["v"].reshape(depth, ndb, dseq, *head_v), sa["s"])
```

```python
import functools
import math

import jax
import jax.numpy as jnp
from jax import lax
from jax.experimental import pallas as pl
from jax.experimental.pallas import tpu as pltpu

F32 = jnp.float32
BF16 = jnp.bfloat16

EPS = 1e-6
CHUNK = 64
GLA_HEADS = 4
GLA_DK_HEAD = 256
GLA_DV_HEAD = 512
GLA_DK = GLA_HEADS * GLA_DK_HEAD
GLA_DV = GLA_HEADS * GLA_DV_HEAD
GLA_RANK = 16
GLA_TAU = 16.0
DIFF_HEADS = 8
DIFF_QK_HEAD = 64
DIFF_V_HEAD = 128
DIFF_QK = DIFF_HEADS * 2 * DIFF_QK_HEAD
DIFF_V = DIFF_HEADS * DIFF_V_HEAD
ROPE_DIM = 16
ROPE_THETA = 500000.0
FFN_RES = 0.5

LANES = 128
VMEM_LIMIT_BYTES = 56 * 1024 * 1024

P_QG = 0
P_KG = P_QG + GLA_DK
P_VG = P_KG + GLA_DK
P_RG = P_VG + GLA_DV
P_QD = P_RG + GLA_DV
P_KD = P_QD + DIFF_QK
P_ZG = P_KD + DIFF_QK
P_ZD = P_ZG + GLA_DV
P_COLS = P_ZD + GLA_DV
W_VD = P_COLS


def _pick_tile(n, candidates):
    for c in candidates:
        if n % c == 0:
            return c
    raise ValueError(f"no tile in {candidates} divides {n}")


def _params(*semantics):
    return pltpu.CompilerParams(dimension_semantics=semantics, vmem_limit_bytes=VMEM_LIMIT_BYTES)


def _nt_dot(a, b):
    return lax.dot_general(a, b, (((1,), (1,)), ((), ())), preferred_element_type=F32)


def _tn_dot(a, b):
    return lax.dot_general(a, b, (((0,), (0,)), ((), ())), preferred_element_type=F32)


def _dot(a, b):
    return jnp.dot(a, b, preferred_element_type=F32)


def _rms(x):
    return x * lax.rsqrt(jnp.mean(x * x, axis=-1, keepdims=True) + EPS)


def _split2(x):
    hi = x.astype(BF16)
    lo = (x - hi.astype(F32)).astype(BF16)
    return hi, lo


def _stacked_out(prev, n_in):
    if prev is None:
        return [], [], {}
    return [prev], [pl.BlockSpec(memory_space=pl.ANY)], {n_in: 0}


def _ffn_body(*refs, tf, f_valid, dn, emit_h):
    if emit_h:
        x_ref, gain_ref, wg_ref, wu_ref, wd_ref, gain2_ref, o_ref, h2_ref, h_ref = refs
    else:
        x_ref, gain_ref, wg_ref, wu_ref, wd_ref, o_ref, h_ref = refs
    j = pl.program_id(1)
    last = pl.num_programs(1) - 1
    d = o_ref.shape[1]

    @pl.when(j == 0)
    def _():
        h_ref[...] = (_rms(x_ref[...]) * gain_ref[...]).astype(BF16)
        o_ref[...] = jnp.zeros_like(o_ref)

    def step(width):
        h = h_ref[...]
        g = _dot(h, wg_ref[:, :width])
        u = _dot(h, wu_ref[:, :width])
        a = (g * jax.nn.sigmoid(g) * u).astype(BF16)
        for n0 in range(0, d, dn):
            o_ref[:, n0:n0 + dn] += _dot(a, wd_ref[:width, n0:n0 + dn])

    @pl.when(j < last)
    def _():
        step(tf)

    @pl.when(j == last)
    def _():
        step(f_valid)
        xn = x_ref[...] + FFN_RES * o_ref[...]
        o_ref[...] = xn
        if emit_h:
            h2_ref[...] = (_rms(xn) * gain2_ref[...]).astype(BF16)


def _ffn(x, gain, wg, wu, wd, layer, gain_next=None):
    t, d = x.shape
    f = wg.shape[2]
    tm = _pick_tile(t, (1024, 512, 256, 128, 64))
    emit_h = gain_next is not None
    tf = 256 if emit_h else 512
    nf = pl.cdiv(f, tf)
    f_valid = f - (nf - 1) * tf
    assert nf >= 2 and f_valid % LANES == 0
    rows = pl.BlockSpec((tm, d), lambda i, j: (i, 0))
    vec = pl.BlockSpec((1, d), lambda i, j: (0, 0))
    in_specs = [
        rows, vec,
        pl.BlockSpec((None, d, tf), lambda i, j: (layer, 0, j)),
        pl.BlockSpec((None, d, tf), lambda i, j: (layer, 0, j)),
        pl.BlockSpec((None, tf, d), lambda i, j: (layer, j, 0)),
    ]
    inputs = [x, gain, wg, wu, wd]
    out_specs = [rows]
    out_shape = [jax.ShapeDtypeStruct((t, d), F32)]
    if emit_h:
        in_specs.append(vec)
        inputs.append(gain_next)
        out_specs.append(rows)
        out_shape.append(jax.ShapeDtypeStruct((t, d), BF16))
    out = pl.pallas_call(
        functools.partial(_ffn_body, tf=tf, f_valid=f_valid, dn=512, emit_h=emit_h),
        grid=(t // tm, nf),
        in_specs=in_specs,
        out_specs=out_specs,
        out_shape=out_shape,
        scratch_shapes=[pltpu.VMEM((tm, d), BF16)],
        compiler_params=_params("parallel", "arbitrary"),
        name="ffn",
    )(*inputs)
    return out if emit_h else out[0]


def _proj_body(h_ref, w_ref, *rest):
    rest[-1][...] = _dot(h_ref[...], w_ref[...])


def _proj(h, w, layer, col0, n_cols, tn, stacked=None):
    t, d = h.shape
    tm = _pick_tile(t, (1024, 512, 256, 128, 64))
    assert col0 % tn == 0 and n_cols % tn == 0
    j0 = col0 // tn
    in_specs = [
        pl.BlockSpec((tm, d), lambda j, i: (i, 0)),
        pl.BlockSpec((None, d, tn), lambda j, i: (layer, 0, j0 + j)),
    ]
    if stacked is None:
        out_spec = pl.BlockSpec((tm, tn), lambda j, i: (i, j))
        out_shape = jax.ShapeDtypeStruct((t, n_cols), F32)
        extra_in, extra_specs, aliases = [], [], {}
    else:
        depth, prev = stacked
        out_spec = pl.BlockSpec((None, tm, tn), lambda j, i: (layer, i, j))
        out_shape = jax.ShapeDtypeStruct((depth, t, n_cols), F32)
        extra_in, extra_specs, aliases = _stacked_out(prev, 2)
    return pl.pallas_call(
        _proj_body,
        grid=(n_cols // tn, t // tm),
        in_specs=in_specs + extra_specs,
        out_specs=out_spec,
        out_shape=out_shape,
        input_output_aliases=aliases,
        compiler_params=_params("parallel", "parallel"),
        name="proj",
    )(h, w, *extra_in)


def _qk_rope_body(x_ref, gain_ref, ones_ref, cos_ref, sl_ref, sh_ref, *rest):
    o_ref = rest[-1]
    half = ROPE_DIM // 2
    cos, sin_lo, sin_hi = cos_ref[...], sl_ref[...], sh_ref[...]
    for c0 in range(0, x_ref.shape[1], LANES):
        x = x_ref[:, c0:c0 + LANES]
        s_hi, s_lo = _split2(x * x)
        ssum = _dot(jnp.concatenate([s_hi, s_lo], axis=1), ones_ref[...])
        y = x * lax.rsqrt(ssum * (1.0 / DIFF_QK_HEAD) + EPS) * gain_ref[:, c0:c0 + LANES]
        out = y * cos + pltpu.roll(y, LANES - half, 1) * sin_lo + pltpu.roll(y, half, 1) * sin_hi
        o_ref[:, c0:c0 + LANES] = out.astype(o_ref.dtype)


def _qk_rope(p, col, gain, group_ones, tables, out_dtype, layer=0, stacked=None):
    t = p.shape[0]
    tm = _pick_tile(t, (1024, 512, 256, 128, 64))
    cos, sin_lo, sin_hi = tables
    period = cos.shape[0] // tm
    const = lambda a: pl.BlockSpec(a.shape, lambda i: (0, 0))
    table = pl.BlockSpec((tm, LANES), lambda i: (i % period, 0))
    in_specs = [pl.BlockSpec((tm, DIFF_QK), lambda i: (i, col // DIFF_QK)), const(gain), const(group_ones),
                table, table, table]
    if stacked is None:
        out_spec = pl.BlockSpec((tm, DIFF_QK), lambda i: (i, 0))
        out_shape = jax.ShapeDtypeStruct((t, DIFF_QK), out_dtype)
        extra_in, extra_specs, aliases = [], [], {}
    else:
        depth, prev = stacked
        out_spec = pl.BlockSpec((None, tm, DIFF_QK), lambda i: (layer, i, 0))
        out_shape = jax.ShapeDtypeStruct((depth, t, DIFF_QK), out_dtype)
        extra_in, extra_specs, aliases = _stacked_out(prev, len(in_specs))
    return pl.pallas_call(
        _qk_rope_body,
        grid=(t // tm,),
        in_specs=in_specs + extra_specs,
        out_specs=out_spec,
        out_shape=out_shape,
        input_output_aliases=aliases,
        compiler_params=_params("parallel"),
        name="qk_rope",
    )(p, gain, group_ones, cos, sin_lo, sin_hi, *extra_in)


def _log_sigmoid(x):
    return jnp.minimum(x, 0.0) - jnp.log1p(jnp.exp(-jnp.abs(x)))


def _gla_body(*refs, n_group, has_s0):
    q_ref, k_ref, v_ref, r_ref, h_ref, wa_ref, w2_ref, ba_ref, gn_ref = refs[:9]
    s0_ref = refs[9] if has_s0 else None
    o_ref, sout_ref, st_ref = refs[-3:]
    c = pl.program_id(1)
    seqs = range(n_group)
    pairs = [(s, hd) for s in seqs for hd in range(GLA_HEADS)]
    ks = lambda hd: slice(hd * GLA_DK_HEAD, (hd + 1) * GLA_DK_HEAD)
    vs = lambda hd: slice(hd * GLA_DV_HEAD, (hd + 1) * GLA_DV_HEAD)

    @pl.when(c == 0)
    def _():
        if has_s0:
            for s, hd in pairs:
                st_ref[s, hd] = s0_ref[s, hd].T
        else:
            st_ref[...] = jnp.zeros_like(st_ref)

    row = lax.broadcasted_iota(jnp.int32, (CHUNK, CHUNK), 0)
    col = lax.broadcasted_iota(jnp.int32, (CHUNK, CHUNK), 1)
    causal = row >= col
    tri = jnp.where(causal, 1.0, 0.0).astype(BF16)

    a = [_dot(h_ref[s], wa_ref[...]).astype(BF16) for s in seqs]
    z = [_dot(a[s], w2_ref[...]) for s in seqs]
    g = [_log_sigmoid(z[s] + ba_ref[...]) * (1.0 / GLA_TAU) for s in seqs]
    g_hi = [g[s].astype(BF16) for s in seqs]
    g_r1 = [g[s] - g_hi[s].astype(F32) for s in seqs]
    g_mid = [g_r1[s].astype(BF16) for s in seqs]
    g_lo = [(g_r1[s] - g_mid[s].astype(F32)).astype(BF16) for s in seqs]
    parts = [[_dot(tri, part[s]) for s in seqs] for part in (g_hi, g_mid, g_lo)]
    b = [parts[0][s] + parts[1][s] + parts[2][s] for s in seqs]
    b_mid = [b[s][CHUNK // 2:CHUNK // 2 + 1] for s in seqs]
    b_end = [b[s][CHUNK - 1:CHUNK] for s in seqs]
    q = [q_ref[s] * (GLA_DK_HEAD ** -0.5) for s in seqs]
    q_in = [(q[s] * jnp.exp(b[s])).astype(BF16) for s in seqs]
    q_rel = [(q[s] * jnp.exp(b[s] - b_mid[s])).astype(BF16) for s in seqs]
    k_rel = [(k_ref[s] * jnp.exp(b_mid[s] - b[s])).astype(BF16) for s in seqs]
    k_end = [(k_ref[s] * jnp.exp(b_end[s] - b[s])).astype(BF16) for s in seqs]
    decay = [jnp.exp(b_end[s]) for s in seqs]

    vb = {(s, hd): v_ref[s, :, vs(hd)].astype(BF16) for s, hd in pairs}
    att = {(s, hd): _nt_dot(q_rel[s][:, ks(hd)], k_rel[s][:, ks(hd)]) for s, hd in pairs}
    st = {(s, hd): st_ref[s, hd] for s, hd in pairs}
    o = {(s, hd): _nt_dot(q_in[s][:, ks(hd)], st[s, hd].astype(BF16)) for s, hd in pairs}
    upd = {(s, hd): _tn_dot(vb[s, hd], k_end[s][:, ks(hd)]) for s, hd in pairs}
    att = {(s, hd): jnp.where(causal, att[s, hd], 0.0).astype(BF16) for s, hd in pairs}
    o = {(s, hd): o[s, hd] + _dot(att[s, hd], vb[s, hd]) for s, hd in pairs}
    for s, hd in pairs:
        st_ref[s, hd] = decay[s][:, ks(hd)] * st[s, hd] + upd[s, hd]
    for s, hd in pairs:
        r = r_ref[s, :, vs(hd)]
        y = _rms(o[s, hd]) * gn_ref[hd:hd + 1, :] * (r * jax.nn.sigmoid(r))
        o_ref[s, :, vs(hd)] = y.astype(BF16)

    @pl.when(c == pl.num_programs(1) - 1)
    def _():
        for s, hd in pairs:
            sout_ref[s, hd] = st_ref[s, hd].T


def _gla(p, h, wa, w2, ba, gn, layer, *, seq_len, states_prev, depth, s0=None):
    t, d = h.shape
    cps = seq_len // CHUNK
    n_seqs = t // seq_len
    n_group = _pick_tile(n_seqs, (4, 2, 1)) if s0 is None else _pick_tile(n_seqs, (2, 1))
    p3 = p.reshape(n_seqs, seq_len, p.shape[1])
    h3 = h.reshape(n_seqs, seq_len, d)
    state_block = (None, n_group, GLA_HEADS, GLA_DK_HEAD, GLA_DV_HEAD)
    state_spec = pl.BlockSpec(state_block, lambda g, c: (layer, g, 0, 0, 0))
    cols = lambda width, col: pl.BlockSpec((n_group, CHUNK, width), lambda g, c: (g, c, col // width))
    in_specs = [
        cols(GLA_DK, P_QG), cols(GLA_DK, P_KG), cols(GLA_DV, P_VG), cols(GLA_DV, P_RG), cols(d, 0),
        pl.BlockSpec((None, d, LANES), lambda g, c: (layer, 0, 0)),
        pl.BlockSpec((None, LANES, GLA_DK), lambda g, c: (layer, 0, 0)),
        pl.BlockSpec((None, 1, GLA_DK), lambda g, c: (layer, 0, 0)),
        pl.BlockSpec((None, GLA_HEADS, GLA_DV_HEAD), lambda g, c: (layer, 0, 0)),
    ]
    inputs = [p3, p3, p3, p3, h3, wa, w2, ba, gn]
    if s0 is not None:
        in_specs.append(state_spec)
        inputs.append(s0)
    extra_in, extra_specs, aliases = _stacked_out(states_prev, len(inputs))
    aliases = {k: 1 for k in aliases}
    out, states = pl.pallas_call(
        functools.partial(_gla_body, n_group=n_group, has_s0=s0 is not None),
        grid=(n_seqs // n_group, cps),
        in_specs=in_specs + extra_specs,
        out_specs=[cols(GLA_DV, 0), state_spec],
        out_shape=[
            jax.ShapeDtypeStruct((n_seqs, seq_len, GLA_DV), BF16),
            jax.ShapeDtypeStruct((depth, n_seqs, GLA_HEADS, GLA_DK_HEAD, GLA_DV_HEAD), F32),
        ],
        input_output_aliases=aliases,
        scratch_shapes=[pltpu.VMEM((n_group, GLA_HEADS, GLA_DV_HEAD, GLA_DK_HEAD), F32)],
        compiler_params=_params("parallel", "arbitrary"),
        name="gla",
    )(*inputs, *extra_in)
    return out.reshape(t, GLA_DV), states


def _lambda_value(lam_ref, lambda_init):
    lp = lam_ref[...]
    d1 = jnp.sum(lp[0:1] * lp[1:2], axis=-1, keepdims=True)
    d2 = jnp.sum(lp[2:3] * lp[3:4], axis=-1, keepdims=True)
    return jnp.exp(d1) - jnp.exp(d2) + lambda_init


def _stack_maps(q):
    q = q.astype(F32) * (DIFF_QK_HEAD ** -0.5)
    lane = lax.broadcasted_iota(jnp.int32, q.shape, 1)
    first = lane < DIFF_QK_HEAD
    return jnp.concatenate([jnp.where(first, q, 0.0), jnp.where(first, 0.0, q)], axis=0).astype(BF16)


def _with_ones_column(v):
    lane = lax.broadcasted_iota(jnp.int32, v.shape, 1)
    return jnp.concatenate([v.astype(BF16), jnp.where(lane == 0, 1.0, 0.0).astype(BF16)], axis=1)


def _diff_attend(scores, values, lam):
    rows = scores[0].shape[0] // 2
    mx = functools.reduce(jnp.maximum, [jnp.max(s, axis=-1, keepdims=True) for s in scores])
    acc = functools.reduce(jnp.add, [_dot(jnp.exp(s - mx).astype(BF16), v) for s, v in zip(scores, values)])
    o = acc[:, :DIFF_V_HEAD] / acc[:, DIFF_V_HEAD:DIFF_V_HEAD + 1]
    return o[:rows] - lam * o[rows:]


def _attn_prompt_body(li_ref, q_ref, k_ref, v_ref, lam_ref, dn_ref, o_ref, *, tq):
    seq_len = q_ref.shape[0]
    kb = k_ref[...].astype(BF16)
    vb = _with_ones_column(v_ref[...])
    lambda_init = li_ref[0]
    lam = _lambda_value(lam_ref, lambda_init)
    row = lax.broadcasted_iota(jnp.int32, (2 * tq, tq), 0) % tq // CHUNK
    col = lax.broadcasted_iota(jnp.int32, (2 * tq, tq), 1) // CHUNK
    visible = row >= col
    def tile_scores(i):
        lo, hi = i * tq, (i + 1) * tq
        qz = _stack_maps(q_ref[lo:hi, :])
        scores = [jnp.where(visible, _nt_dot(qz, kb[lo:hi]), -jnp.inf)]
        if i > 0:
            scores.append(_nt_dot(qz, kb[:lo]))
        return scores

    n_tiles = seq_len // tq
    scores = tile_scores(0)
    for i in range(n_tiles):
        lo, hi = i * tq, (i + 1) * tq
        ahead = tile_scores(i + 1) if i + 1 < n_tiles else None
        values = [vb[lo:hi]] + ([vb[:lo]] if i > 0 else [])
        o = _diff_attend(scores, values, lam)
        o_ref[lo:hi, :] = (_rms(o) * dn_ref[...] * (1.0 - lambda_init)).astype(BF16)
        scores = ahead


def _attn_prompt(qd, k_all, v_all, li, lam, dn, layer, *, seq_len):
    t = qd.shape[0]
    n_seqs = t // seq_len
    tq = _pick_tile(seq_len, (256, 128, 64))
    dv = DIFF_V_HEAD
    kv = pl.BlockSpec((None, seq_len, dv), lambda b, h: (layer, b, h))
    return pl.pallas_call(
        functools.partial(_attn_prompt_body, tq=tq),
        grid=(n_seqs, DIFF_HEADS),
        in_specs=[
            pl.BlockSpec(memory_space=pltpu.SMEM),
            pl.BlockSpec((seq_len, dv), lambda b, h: (b, h)),
            kv, kv,
            pl.BlockSpec((None, 4, DIFF_QK_HEAD), lambda b, h: (layer, 0, 0)),
            pl.BlockSpec((None, None, 1, dv), lambda b, h: (layer, h, 0, 0)),
        ],
        out_specs=pl.BlockSpec((seq_len, dv), lambda b, h: (b, h)),
        out_shape=jax.ShapeDtypeStruct((t, DIFF_V), BF16),
        compiler_params=_params("parallel", "parallel"),
        name="attn_prompt",
    )(li, qd, k_all, v_all, lam, dn)


def _attn_sample_body(li_ref, q_ref, kc_ref, vc_ref, kn_ref, vn_ref, lam_ref, dn_ref, o_ref):
    past = kc_ref.shape[0] // DIFF_HEADS
    lambda_init = li_ref[0]
    lam = _lambda_value(lam_ref, lambda_init)
    cols = lambda h: slice(h * DIFF_V_HEAD, (h + 1) * DIFF_V_HEAD)
    head_rows = lambda h: pl.ds(h, past, stride=DIFF_HEADS)

    def head_scores(h):
        qz = _stack_maps(q_ref[:, cols(h)])
        return [_nt_dot(qz, kc_ref[head_rows(h), :].astype(BF16)), _nt_dot(qz, kn_ref[:, cols(h)].astype(BF16))]

    scores = head_scores(0)
    for h in range(DIFF_HEADS):
        ahead = head_scores(h + 1) if h + 1 < DIFF_HEADS else None
        values = [_with_ones_column(vc_ref[head_rows(h), :]), _with_ones_column(vn_ref[:, cols(h)])]
        o = _diff_attend(scores, values, lam)
        o_ref[:, cols(h)] = (_rms(o) * dn_ref[h] * (1.0 - lambda_init)).astype(BF16)
        scores = ahead


def _attn_sample(qd, k_all, v_all, cache_k, cache_v, li, lam, dn, layer, *, seq_len):
    t = qd.shape[0]
    n_seqs = t // seq_len
    depth, _, past, heads, dv = cache_k.shape
    rows = past * heads
    new = pl.BlockSpec((None, seq_len, heads * dv), lambda b: (layer, b, 0))
    old = pl.BlockSpec((None, None, rows, dv), lambda b: (layer, b, 0, 0))
    return pl.pallas_call(
        _attn_sample_body,
        grid=(n_seqs,),
        in_specs=[
            pl.BlockSpec(memory_space=pltpu.SMEM),
            pl.BlockSpec((seq_len, heads * dv), lambda b: (b, 0)),
            old, old, new, new,
            pl.BlockSpec((None, 4, DIFF_QK_HEAD), lambda b: (layer, 0, 0)),
            pl.BlockSpec((None, heads, 1, dv), lambda b: (layer, 0, 0, 0)),
        ],
        out_specs=pl.BlockSpec((seq_len, heads * dv), lambda b: (b, 0)),
        out_shape=jax.ShapeDtypeStruct((t, DIFF_V), BF16),
        compiler_params=_params("parallel"),
        name="attn_sample",
    )(li, qd, cache_k.reshape(depth, n_seqs, rows, dv), cache_v.reshape(depth, n_seqs, rows, dv),
      k_all, v_all, lam, dn)


def _mix_out_body(x_ref, ag_ref, ad_ref, zg_ref, zd_ref, wgo_ref, wdo_ref, wo_ref, o_ref):
    yg = _dot(ag_ref[...], wgo_ref[...])
    yd = _dot(ad_ref[...], wdo_ref[...])
    m = jax.nn.sigmoid(zg_ref[...]) * yg + jax.nn.sigmoid(zd_ref[...]) * yd
    o_ref[...] = x_ref[...] + _dot(m.astype(BF16), wo_ref[...])


def _mix_out(x, ag, ad, p, wgo, wdo, wo, layer):
    t, d = x.shape
    tm = _pick_tile(t, (256, 128, 64))
    rows = lambda width, col=0: pl.BlockSpec((tm, width), lambda i: (i, col // width))
    resident = lambda w: pl.BlockSpec((None,) + w.shape[1:], lambda i: (layer, 0, 0),
                                      pipeline_mode=pl.Buffered(1))
    return pl.pallas_call(
        _mix_out_body,
        grid=(t // tm,),
        in_specs=[
            rows(d), rows(GLA_DV), rows(DIFF_V), rows(d, P_ZG), rows(d, P_ZG + d),
            resident(wgo), resident(wdo), resident(wo),
        ],
        out_specs=rows(d),
        out_shape=jax.ShapeDtypeStruct((t, d), F32),
        compiler_params=_params("parallel"),
        name="mix_out",
    )(x, ag, ad, p, p, wgo, wdo, wo)


def _rope_tables(pos):
    half = ROPE_DIM // 2
    inv = ROPE_THETA ** (-jnp.arange(half, dtype=F32) * 2.0 / ROPE_DIM)
    ang = pos.astype(F32)[:, None] * inv[None, :]
    cos, sin = jnp.cos(ang), jnp.sin(ang)
    n = pos.shape[0]
    rest = DIFF_QK_HEAD - ROPE_DIM
    zeros = jnp.zeros((n, half), F32)
    pad = lambda parts, fill: jnp.tile(
        jnp.concatenate(parts + [jnp.full((n, rest), fill, F32)], axis=1), (1, LANES // DIFF_QK_HEAD))
    return pad([cos, cos], 1.0), pad([-sin, zeros], 0.0), pad([zeros, sin], 0.0)


def kernel(x_prompt, x_sample, cache_diff_k, cache_diff_v, state_gla, norm_ffn1, w_ffn1_gate, w_ffn1_up,
           w_ffn1_down, norm_mix, w_in, w_alpha2, b_alpha, q_norm, k_norm, diff_lambda, gla_norm, diff_norm,
           w_gla_o, w_diff_o, w_out, norm_ffn2, w_ffn2_gate, w_ffn2_up, w_ffn2_down):
    nb, seq, d = x_prompt.shape
    ndb, dseq, _ = x_sample.shape
    depth = w_in.shape[0]
    past = cache_diff_k.shape[2]
    assert seq % CHUNK == 0 and dseq == CHUNK and d == GLA_DV

    bf = lambda w: w.astype(BF16)
    wg1, wu1, wd1 = bf(w_ffn1_gate), bf(w_ffn1_up), bf(w_ffn1_down)
    wg2, wu2, wd2 = bf(w_ffn2_gate), bf(w_ffn2_up), bf(w_ffn2_down)
    a0 = P_QD
    qd0 = a0 + GLA_RANK
    vd0 = qd0 + 2 * DIFF_QK
    zg0 = vd0 + DIFF_V
    w_main = jnp.concatenate([w_in[:, :, :a0], w_in[:, :, qd0:vd0], w_in[:, :, zg0:], w_in[:, :, vd0:zg0]],
                             axis=-1).astype(BF16)
    w_a = jnp.pad(w_in[:, :, a0:a0 + GLA_RANK], ((0, 0), (0, 0), (0, LANES - GLA_RANK))).astype(BF16)
    w_2 = jnp.pad(w_alpha2, ((0, 0), (0, LANES - GLA_RANK), (0, 0))).astype(BF16)
    b_a = b_alpha.reshape(depth, 1, GLA_DK)
    wgo, wdo, wo = bf(w_gla_o), bf(w_diff_o), bf(w_out)
    n_groups = DIFF_QK // DIFF_QK_HEAD
    lane_group = (jnp.arange(2 * LANES) % LANES) // DIFF_QK_HEAD
    group_ones = (lane_group[:, None] == lane_group[None, :LANES]).astype(BF16)
    qn = jnp.tile(q_norm, (1, n_groups)).reshape(depth, 1, DIFF_QK)
    kn = jnp.tile(k_norm, (1, n_groups)).reshape(depth, 1, DIFF_QK)
    dn = diff_norm.reshape(depth, DIFF_HEADS, 1, DIFF_V_HEAD)

    streams = []
    for x0, n_seqs, slen, start in ((x_prompt, nb, seq, 0), (x_sample, ndb, dseq, past)):
        rows = n_seqs * slen
        tm = _pick_tile(rows, (1024, 512, 256, 128, 64))
        period = max(slen, tm)
        assert period % slen == 0 and period % tm == 0
        tables = _rope_tables(start + jnp.arange(period) % slen)
        streams.append(dict(x=x0.reshape(rows, d), slen=slen, tables=tables, k=None, v=None, s=None))

    for l in range(depth):
        lambda_init = 0.8 - 0.6 * math.exp(-0.3 * l)
        li = jnp.full((1,), lambda_init, F32)
        for si, st in enumerate(streams):
            slen = st["slen"]
            x, h = _ffn(st["x"], norm_ffn1[l][None], wg1, wu1, wd1, l, gain_next=norm_mix[l][None])
            p = _proj(h, w_main, l, 0, P_COLS, 2048)
            st["v"] = _proj(h, w_main, l, W_VD, DIFF_V, DIFF_V, stacked=(depth, st["v"]))
            qd = _qk_rope(p, P_QD, qn[l], group_ones, st["tables"], BF16)
            st["k"] = _qk_rope(p, P_KD, kn[l], group_ones, st["tables"], F32, layer=l, stacked=(depth, st["k"]))
            ag, st["s"] = _gla(p, h, w_a, w_2, b_a, gla_norm, l, seq_len=slen, states_prev=st["s"],
                               depth=depth, s0=state_gla if si == 1 else None)
            if si == 0:
                ad = _attn_prompt(qd, st["k"], st["v"], li, diff_lambda, dn, l, seq_len=slen)
            else:
                ad = _attn_sample(qd, st["k"], st["v"], cache_diff_k, cache_diff_v, li, diff_lambda, dn, l,
                                  seq_len=slen)
            x = _mix_out(x, ag, ad, p, wgo, wdo, wo, l)
            st["x"] = _ffn(x, norm_ffn2[l][None], wg2, wu2, wd2, l)

    pr, sa = streams
    head_k = (DIFF_HEADS, 2 * DIFF_QK_HEAD)
    head_v = (DIFF_HEADS, DIFF_V_HEAD)
    return (pr["x"].reshape(nb, seq, d), sa["x"].reshape(ndb, dseq, d),
            pr["k"].reshape(depth, nb, seq, *head_k), pr["v"].reshape(depth, nb, seq, *head_v), pr["s"],
            sa["k"].reshape(depth, ndb, dseq, *head_k), sa["v"].reshape(depth, ndb, dseq, *head_v), sa["s"])
```

```python
import functools
import math

import jax
import jax.numpy as jnp
from jax import lax
from jax.experimental import pallas as pl
from jax.experimental.pallas import tpu as pltpu

F32 = jnp.float32
BF16 = jnp.bfloat16

EPS = 1e-6
CHUNK = 64
GLA_HEADS = 4
GLA_DK_HEAD = 256
GLA_DV_HEAD = 512
GLA_DK = GLA_HEADS * GLA_DK_HEAD
GLA_DV = GLA_HEADS * GLA_DV_HEAD
GLA_RANK = 16
GLA_TAU = 16.0
DIFF_HEADS = 8
DIFF_QK_HEAD = 64
DIFF_V_HEAD = 128
DIFF_QK = DIFF_HEADS * 2 * DIFF_QK_HEAD
DIFF_V = DIFF_HEADS * DIFF_V_HEAD
ROPE_DIM = 16
ROPE_THETA = 500000.0
FFN_RES = 0.5

LANES = 128
VMEM_BYTES = 64 * 1024 * 1024
VMEM_LIMIT_BYTES = VMEM_BYTES - 4 * 1024 * 1024

P_QG = 0
P_KG = P_QG + GLA_DK
P_VG = P_KG + GLA_DK
P_RG = P_VG + GLA_DV
P_QD = P_RG + GLA_DV
P_KD = P_QD + DIFF_QK
P_ZG = P_KD + DIFF_QK
P_ZD = P_ZG + GLA_DV
P_COLS = P_ZD + GLA_DV
W_VD = P_COLS


def _pick_tile(n, candidates):
    for c in candidates:
        if n % c == 0:
            return c
    raise ValueError(f"no tile in {candidates} divides {n}")


def _params(*semantics):
    return pltpu.CompilerParams(dimension_semantics=semantics, vmem_limit_bytes=VMEM_LIMIT_BYTES)


def _nt_dot(a, b):
    return lax.dot_general(a, b, (((1,), (1,)), ((), ())), preferred_element_type=F32)


def _tn_dot(a, b):
    return lax.dot_general(a, b, (((0,), (0,)), ((), ())), preferred_element_type=F32)


def _dot(a, b):
    return jnp.dot(a, b, preferred_element_type=F32)


def _rms(x):
    return x * lax.rsqrt(jnp.mean(x * x, axis=-1, keepdims=True) + EPS)


def _split2(x):
    hi = x.astype(BF16)
    lo = (x - hi.astype(F32)).astype(BF16)
    return hi, lo


def _stacked_out(prev, n_in):
    if prev is None:
        return [], [], {}
    return [prev], [pl.BlockSpec(memory_space=pl.ANY)], {n_in: 0}


def _ffn_body(*refs, tf, f_valid, dn, emit_h):
    if emit_h:
        x_ref, gain_ref, wg_ref, wu_ref, wd_ref, gain2_ref, o_ref, h2_ref, h_ref = refs
    else:
        x_ref, gain_ref, wg_ref, wu_ref, wd_ref, o_ref, h_ref = refs
    j = pl.program_id(1)
    last = pl.num_programs(1) - 1
    d = o_ref.shape[1]

    def step(width, where):
        h = h_ref[...]
        g = _dot(h, wg_ref[:, :width])
        u = _dot(h, wu_ref[:, :width])
        a = (g * jax.nn.sigmoid(g) * u).astype(BF16)
        for n0 in range(0, d, dn):
            cols = slice(n0, n0 + dn)
            y = _dot(a, wd_ref[:width, cols])
            if where == "first":
                o_ref[:, cols] = y
            elif where == "middle":
                o_ref[:, cols] += y
            else:
                o_ref[:, cols] = x_ref[:, cols] + FFN_RES * (o_ref[:, cols] + y)

    @pl.when(j == 0)
    def _():
        h_ref[...] = (_rms(x_ref[...]) * gain_ref[...]).astype(BF16)
        step(tf, "first")

    @pl.when(jnp.logical_and(j > 0, j < last))
    def _():
        step(tf, "middle")

    @pl.when(j == last)
    def _():
        step(f_valid, "last")
        if emit_h:
            h2_ref[...] = (_rms(o_ref[...]) * gain2_ref[...]).astype(BF16)


def _ffn(x, gain, wg, wu, wd, layer, gain_next=None):
    t, d = x.shape
    f = wg.shape[2]
    tm = _pick_tile(t, (1024, 512, 256, 128, 64))
    emit_h = gain_next is not None
    tf = 512
    nf = pl.cdiv(f, tf)
    f_valid = f - (nf - 1) * tf
    assert nf >= 2 and f_valid % LANES == 0
    rows = pl.BlockSpec((tm, d), lambda i, j: (i, 0))
    vec = pl.BlockSpec((1, d), lambda i, j: (0, 0))
    in_specs = [
        rows, vec,
        pl.BlockSpec((None, d, tf), lambda i, j: (layer, 0, j)),
        pl.BlockSpec((None, d, tf), lambda i, j: (layer, 0, j)),
        pl.BlockSpec((None, tf, d), lambda i, j: (layer, j, 0)),
    ]
    inputs = [x, gain, wg, wu, wd]
    out_specs = [rows]
    out_shape = [jax.ShapeDtypeStruct((t, d), F32)]
    if emit_h:
        in_specs.append(vec)
        inputs.append(gain_next)
        out_specs.append(pl.BlockSpec((tm, d), lambda i, j: (i, 0), pipeline_mode=pl.Buffered(1)))
        out_shape.append(jax.ShapeDtypeStruct((t, d), BF16))
    out = pl.pallas_call(
        functools.partial(_ffn_body, tf=tf, f_valid=f_valid, dn=512, emit_h=emit_h),
        grid=(t // tm, nf),
        in_specs=in_specs,
        out_specs=out_specs,
        out_shape=out_shape,
        scratch_shapes=[pltpu.VMEM((tm, d), BF16)],
        compiler_params=_params("parallel", "arbitrary"),
        name="ffn",
    )(*inputs)
    return out if emit_h else out[0]


def _proj_body(h_ref, w_ref, *rest):
    rest[-1][...] = _dot(h_ref[...], w_ref[...])


def _proj(h, w, layer, col0, n_cols, tn, stacked=None):
    t, d = h.shape
    tm = _pick_tile(t, (1024, 512, 256, 128, 64))
    assert col0 % tn == 0 and n_cols % tn == 0
    j0 = col0 // tn
    in_specs = [
        pl.BlockSpec((tm, d), lambda j, i: (i, 0)),
        pl.BlockSpec((None, d, tn), lambda j, i: (layer, 0, j0 + j)),
    ]
    if stacked is None:
        out_spec = pl.BlockSpec((tm, tn), lambda j, i: (i, j))
        out_shape = jax.ShapeDtypeStruct((t, n_cols), F32)
        extra_in, extra_specs, aliases = [], [], {}
    else:
        depth, prev = stacked
        out_spec = pl.BlockSpec((None, tm, tn), lambda j, i: (layer, i, j))
        out_shape = jax.ShapeDtypeStruct((depth, t, n_cols), F32)
        extra_in, extra_specs, aliases = _stacked_out(prev, 2)
    return pl.pallas_call(
        _proj_body,
        grid=(n_cols // tn, t // tm),
        in_specs=in_specs + extra_specs,
        out_specs=out_spec,
        out_shape=out_shape,
        input_output_aliases=aliases,
        compiler_params=_params("parallel", "parallel"),
        name="proj",
    )(h, w, *extra_in)


def _qk_rope_body(x_ref, gain_ref, ones_ref, cos_ref, sl_ref, sh_ref, *rest):
    o_ref = rest[-1]
    half = ROPE_DIM // 2
    cos, sin_lo, sin_hi = cos_ref[...], sl_ref[...], sh_ref[...]
    for c0 in range(0, x_ref.shape[1], LANES):
        x = x_ref[:, c0:c0 + LANES]
        s_hi, s_lo = _split2(x * x)
        ssum = _dot(jnp.concatenate([s_hi, s_lo], axis=1), ones_ref[...])
        y = x * lax.rsqrt(ssum * (1.0 / DIFF_QK_HEAD) + EPS) * gain_ref[:, c0:c0 + LANES]
        out = y * cos + pltpu.roll(y, LANES - half, 1) * sin_lo + pltpu.roll(y, half, 1) * sin_hi
        o_ref[:, c0:c0 + LANES] = out.astype(o_ref.dtype)


def _qk_rope(p, col, gain, group_ones, tables, out_dtype, layer=0, stacked=None):
    t = p.shape[0]
    tm = _pick_tile(t, (1024, 512, 256, 128, 64))
    cos, sin_lo, sin_hi = tables
    period = cos.shape[0] // tm
    const = lambda a: pl.BlockSpec(a.shape, lambda i: (0, 0))
    table = pl.BlockSpec((tm, LANES), lambda i: (i % period, 0))
    in_specs = [pl.BlockSpec((tm, DIFF_QK), lambda i: (i, col // DIFF_QK)), const(gain), const(group_ones),
                table, table, table]
    if stacked is None:
        out_spec = pl.BlockSpec((tm, DIFF_QK), lambda i: (i, 0))
        out_shape = jax.ShapeDtypeStruct((t, DIFF_QK), out_dtype)
        extra_in, extra_specs, aliases = [], [], {}
    else:
        depth, prev = stacked
        out_spec = pl.BlockSpec((None, tm, DIFF_QK), lambda i: (layer, i, 0))
        out_shape = jax.ShapeDtypeStruct((depth, t, DIFF_QK), out_dtype)
        extra_in, extra_specs, aliases = _stacked_out(prev, len(in_specs))
    return pl.pallas_call(
        _qk_rope_body,
        grid=(t // tm,),
        in_specs=in_specs + extra_specs,
        out_specs=out_spec,
        out_shape=out_shape,
        input_output_aliases=aliases,
        compiler_params=_params("parallel"),
        name="qk_rope",
    )(p, gain, group_ones, cos, sin_lo, sin_hi, *extra_in)


def _log_sigmoid(x):
    return jnp.minimum(x, 0.0) - jnp.log1p(jnp.exp(-jnp.abs(x)))


def _gla_body(*refs, n_group, has_s0):
    q_ref, k_ref, v_ref, r_ref, h_ref, wa_ref, w2_ref, ba_ref, gn_ref = refs[:9]
    s0_ref = refs[9] if has_s0 else None
    o_ref, sout_ref, st_ref = refs[-3:]
    c = pl.program_id(1)
    seqs = range(n_group)
    pairs = [(s, hd) for s in seqs for hd in range(GLA_HEADS)]
    ks = lambda hd: slice(hd * GLA_DK_HEAD, (hd + 1) * GLA_DK_HEAD)
    vs = lambda hd: slice(hd * GLA_DV_HEAD, (hd + 1) * GLA_DV_HEAD)

    @pl.when(c == 0)
    def _():
        if has_s0:
            for s, hd in pairs:
                st_ref[s, hd] = s0_ref[s, hd].T
        else:
            st_ref[...] = jnp.zeros_like(st_ref)

    row = lax.broadcasted_iota(jnp.int32, (CHUNK, CHUNK), 0)
    col = lax.broadcasted_iota(jnp.int32, (CHUNK, CHUNK), 1)
    causal = row >= col
    tri = jnp.where(causal, 1.0, 0.0).astype(BF16)

    a = [_dot(h_ref[s], wa_ref[...]).astype(BF16) for s in seqs]
    z = [_dot(a[s], w2_ref[...]) for s in seqs]
    g = [_log_sigmoid(z[s] + ba_ref[...]) * (1.0 / GLA_TAU) for s in seqs]
    g_hi = [g[s].astype(BF16) for s in seqs]
    g_r1 = [g[s] - g_hi[s].astype(F32) for s in seqs]
    g_mid = [g_r1[s].astype(BF16) for s in seqs]
    g_lo = [(g_r1[s] - g_mid[s].astype(F32)).astype(BF16) for s in seqs]
    parts = [[_dot(tri, part[s]) for s in seqs] for part in (g_hi, g_mid, g_lo)]
    b = [parts[0][s] + parts[1][s] + parts[2][s] for s in seqs]
    b_mid = [b[s][CHUNK // 2:CHUNK // 2 + 1] for s in seqs]
    b_end = [b[s][CHUNK - 1:CHUNK] for s in seqs]
    q = [q_ref[s] * (GLA_DK_HEAD ** -0.5) for s in seqs]
    q_in = [(q[s] * jnp.exp(b[s])).astype(BF16) for s in seqs]
    q_rel = [(q[s] * jnp.exp(b[s] - b_mid[s])).astype(BF16) for s in seqs]
    k_rel = [(k_ref[s] * jnp.exp(b_mid[s] - b[s])).astype(BF16) for s in seqs]
    k_end = [(k_ref[s] * jnp.exp(b_end[s] - b[s])).astype(BF16) for s in seqs]
    decay = [jnp.exp(b_end[s]) for s in seqs]

    vb = {(s, hd): v_ref[s, :, vs(hd)].astype(BF16) for s, hd in pairs}
    att = {(s, hd): _nt_dot(q_rel[s][:, ks(hd)], k_rel[s][:, ks(hd)]) for s, hd in pairs}
    st = {(s, hd): st_ref[s, hd] for s, hd in pairs}
    o = {(s, hd): _nt_dot(q_in[s][:, ks(hd)], st[s, hd].astype(BF16)) for s, hd in pairs}
    upd = {(s, hd): _tn_dot(vb[s, hd], k_end[s][:, ks(hd)]) for s, hd in pairs}
    att = {(s, hd): jnp.where(causal, att[s, hd], 0.0).astype(BF16) for s, hd in pairs}
    o = {(s, hd): o[s, hd] + _dot(att[s, hd], vb[s, hd]) for s, hd in pairs}
    for s, hd in pairs:
        st_ref[s, hd] = decay[s][:, ks(hd)] * st[s, hd] + upd[s, hd]
    for s, hd in pairs:
        r = r_ref[s, :, vs(hd)]
        y = _rms(o[s, hd]) * gn_ref[hd:hd + 1, :] * (r * jax.nn.sigmoid(r))
        o_ref[s, :, vs(hd)] = y.astype(BF16)

    @pl.when(c == pl.num_programs(1) - 1)
    def _():
        for s, hd in pairs:
            sout_ref[s, hd] = st_ref[s, hd].T


def _gla(p, h, wa, w2, ba, gn, layer, *, seq_len, states_prev, depth, s0=None):
    t, d = h.shape
    cps = seq_len // CHUNK
    n_seqs = t // seq_len
    n_group = _pick_tile(n_seqs, (4, 2, 1)) if s0 is None else _pick_tile(n_seqs, (2, 1))
    p3 = p.reshape(n_seqs, seq_len, p.shape[1])
    h3 = h.reshape(n_seqs, seq_len, d)
    state_block = (None, n_group, GLA_HEADS, GLA_DK_HEAD, GLA_DV_HEAD)
    state_spec = pl.BlockSpec(state_block, lambda g, c: (layer, g, 0, 0, 0))
    cols = lambda width, col: pl.BlockSpec((n_group, CHUNK, width), lambda g, c: (g, c, col // width))
    in_specs = [
        cols(GLA_DK, P_QG), cols(GLA_DK, P_KG), cols(GLA_DV, P_VG), cols(GLA_DV, P_RG), cols(d, 0),
        pl.BlockSpec((None, d, LANES), lambda g, c: (layer, 0, 0)),
        pl.BlockSpec((None, LANES, GLA_DK), lambda g, c: (layer, 0, 0)),
        pl.BlockSpec((None, 1, GLA_DK), lambda g, c: (layer, 0, 0)),
        pl.BlockSpec((None, GLA_HEADS, GLA_DV_HEAD), lambda g, c: (layer, 0, 0)),
    ]
    inputs = [p3, p3, p3, p3, h3, wa, w2, ba, gn]
    if s0 is not None:
        in_specs.append(state_spec)
        inputs.append(s0)
    extra_in, extra_specs, aliases = _stacked_out(states_prev, len(inputs))
    aliases = {k: 1 for k in aliases}
    out, states = pl.pallas_call(
        functools.partial(_gla_body, n_group=n_group, has_s0=s0 is not None),
        grid=(n_seqs // n_group, cps),
        in_specs=in_specs + extra_specs,
        out_specs=[cols(GLA_DV, 0), state_spec],
        out_shape=[
            jax.ShapeDtypeStruct((n_seqs, seq_len, GLA_DV), BF16),
            jax.ShapeDtypeStruct((depth, n_seqs, GLA_HEADS, GLA_DK_HEAD, GLA_DV_HEAD), F32),
        ],
        input_output_aliases=aliases,
        scratch_shapes=[pltpu.VMEM((n_group, GLA_HEADS, GLA_DV_HEAD, GLA_DK_HEAD), F32)],
        compiler_params=_params("parallel", "arbitrary"),
        name="gla",
    )(*inputs, *extra_in)
    return out.reshape(t, GLA_DV), states


def _lambda_value(lam_ref, lambda_init):
    lp = lam_ref[...]
    d1 = jnp.sum(lp[0:1] * lp[1:2], axis=-1, keepdims=True)
    d2 = jnp.sum(lp[2:3] * lp[3:4], axis=-1, keepdims=True)
    return jnp.exp(d1) - jnp.exp(d2) + lambda_init


def _stack_maps(q):
    q = q.astype(F32) * (DIFF_QK_HEAD ** -0.5)
    lane = lax.broadcasted_iota(jnp.int32, q.shape, 1)
    first = lane < DIFF_QK_HEAD
    return jnp.concatenate([jnp.where(first, q, 0.0), jnp.where(first, 0.0, q)], axis=0).astype(BF16)


def _with_ones_column(v):
    lane = lax.broadcasted_iota(jnp.int32, v.shape, 1)
    return jnp.concatenate([v.astype(BF16), jnp.where(lane == 0, 1.0, 0.0).astype(BF16)], axis=1)


def _diff_attend(scores, values, lam):
    rows = scores[0].shape[0] // 2
    mx = functools.reduce(jnp.maximum, [jnp.max(s, axis=-1, keepdims=True) for s in scores])
    acc = functools.reduce(jnp.add, [_dot(jnp.exp(s - mx).astype(BF16), v) for s, v in zip(scores, values)])
    o = acc[:, :DIFF_V_HEAD] / acc[:, DIFF_V_HEAD:DIFF_V_HEAD + 1]
    return o[:rows] - lam * o[rows:]


def _attn_prompt_body(li_ref, q_ref, k_ref, v_ref, lam_ref, dn_ref, o_ref, *, tq):
    seq_len = q_ref.shape[0]
    kb = k_ref[...].astype(BF16)
    vb = _with_ones_column(v_ref[...])
    lambda_init = li_ref[0]
    lam = _lambda_value(lam_ref, lambda_init)
    row = lax.broadcasted_iota(jnp.int32, (2 * tq, tq), 0) % tq // CHUNK
    col = lax.broadcasted_iota(jnp.int32, (2 * tq, tq), 1) // CHUNK
    visible = row >= col
    def tile_scores(i):
        lo, hi = i * tq, (i + 1) * tq
        qz = _stack_maps(q_ref[lo:hi, :])
        scores = [jnp.where(visible, _nt_dot(qz, kb[lo:hi]), -jnp.inf)]
        if i > 0:
            scores.append(_nt_dot(qz, kb[:lo]))
        return scores

    n_tiles = seq_len // tq
    scores = tile_scores(0)
    for i in range(n_tiles):
        lo, hi = i * tq, (i + 1) * tq
        ahead = tile_scores(i + 1) if i + 1 < n_tiles else None
        values = [vb[lo:hi]] + ([vb[:lo]] if i > 0 else [])
        o = _diff_attend(scores, values, lam)
        o_ref[lo:hi, :] = (_rms(o) * dn_ref[...] * (1.0 - lambda_init)).astype(BF16)
        scores = ahead


def _attn_prompt(qd, k_all, v_all, li, lam, dn, layer, *, seq_len):
    t = qd.shape[0]
    n_seqs = t // seq_len
    tq = _pick_tile(seq_len, (256, 128, 64))
    dv = DIFF_V_HEAD
    kv = pl.BlockSpec((None, seq_len, dv), lambda b, h: (layer, b, h))
    return pl.pallas_call(
        functools.partial(_attn_prompt_body, tq=tq),
        grid=(n_seqs, DIFF_HEADS),
        in_specs=[
            pl.BlockSpec(memory_space=pltpu.SMEM),
            pl.BlockSpec((seq_len, dv), lambda b, h: (b, h)),
            kv, kv,
            pl.BlockSpec((None, 4, DIFF_QK_HEAD), lambda b, h: (layer, 0, 0)),
            pl.BlockSpec((None, None, 1, dv), lambda b, h: (layer, h, 0, 0)),
        ],
        out_specs=pl.BlockSpec((seq_len, dv), lambda b, h: (b, h)),
        out_shape=jax.ShapeDtypeStruct((t, DIFF_V), BF16),
        compiler_params=_params("parallel", "parallel"),
        name="attn_prompt",
    )(li, qd, k_all, v_all, lam, dn)


def _attn_sample_body(li_ref, q_ref, kc_ref, vc_ref, kn_ref, vn_ref, lam_ref, dn_ref, o_ref):
    past = kc_ref.shape[0] // DIFF_HEADS
    lambda_init = li_ref[0]
    lam = _lambda_value(lam_ref, lambda_init)
    cols = lambda h: slice(h * DIFF_V_HEAD, (h + 1) * DIFF_V_HEAD)
    head_rows = lambda h: pl.ds(h, past, stride=DIFF_HEADS)

    def head_scores(h):
        qz = _stack_maps(q_ref[:, cols(h)])
        return [_nt_dot(qz, kc_ref[head_rows(h), :].astype(BF16)), _nt_dot(qz, kn_ref[:, cols(h)].astype(BF16))]

    scores = head_scores(0)
    for h in range(DIFF_HEADS):
        ahead = head_scores(h + 1) if h + 1 < DIFF_HEADS else None
        values = [_with_ones_column(vc_ref[head_rows(h), :]), _with_ones_column(vn_ref[:, cols(h)])]
        o = _diff_attend(scores, values, lam)
        o_ref[:, cols(h)] = (_rms(o) * dn_ref[h] * (1.0 - lambda_init)).astype(BF16)
        scores = ahead


def _attn_sample(qd, k_all, v_all, cache_k, cache_v, li, lam, dn, layer, *, seq_len):
    t = qd.shape[0]
    n_seqs = t // seq_len
    depth, _, past, heads, dv = cache_k.shape
    rows = past * heads
    new = pl.BlockSpec((None, seq_len, heads * dv), lambda b: (layer, b, 0))
    old = pl.BlockSpec((None, None, rows, dv), lambda b: (layer, b, 0, 0))
    return pl.pallas_call(
        _attn_sample_body,
        grid=(n_seqs,),
        in_specs=[
            pl.BlockSpec(memory_space=pltpu.SMEM),
            pl.BlockSpec((seq_len, heads * dv), lambda b: (b, 0)),
            old, old, new, new,
            pl.BlockSpec((None, 4, DIFF_QK_HEAD), lambda b: (layer, 0, 0)),
            pl.BlockSpec((None, heads, 1, dv), lambda b: (layer, 0, 0, 0)),
        ],
        out_specs=pl.BlockSpec((seq_len, heads * dv), lambda b: (b, 0)),
        out_shape=jax.ShapeDtypeStruct((t, DIFF_V), BF16),
        compiler_params=_params("parallel"),
        name="attn_sample",
    )(li, qd, cache_k.reshape(depth, n_seqs, rows, dv), cache_v.reshape(depth, n_seqs, rows, dv),
      k_all, v_all, lam, dn)


def _mix_out_body(x_ref, ag_ref, ad_ref, zg_ref, zd_ref, wgo_ref, wdo_ref, wo_ref, o_ref):
    yg = _dot(ag_ref[...], wgo_ref[...])
    yd = _dot(ad_ref[...], wdo_ref[...])
    m = jax.nn.sigmoid(zg_ref[...]) * yg + jax.nn.sigmoid(zd_ref[...]) * yd
    o_ref[...] = x_ref[...] + _dot(m.astype(BF16), wo_ref[...])


def _mix_out(x, ag, ad, p, wgo, wdo, wo, layer):
    t, d = x.shape
    tm = _pick_tile(t, (256, 128, 64))
    rows = lambda width, col=0: pl.BlockSpec((tm, width), lambda i: (i, col // width))
    resident = lambda w: pl.BlockSpec((None,) + w.shape[1:], lambda i: (layer, 0, 0),
                                      pipeline_mode=pl.Buffered(1))
    return pl.pallas_call(
        _mix_out_body,
        grid=(t // tm,),
        in_specs=[
            rows(d), rows(GLA_DV), rows(DIFF_V), rows(d, P_ZG), rows(d, P_ZG + d),
            resident(wgo), resident(wdo), resident(wo),
        ],
        out_specs=rows(d),
        out_shape=jax.ShapeDtypeStruct((t, d), F32),
        compiler_params=_params("parallel"),
        name="mix_out",
    )(x, ag, ad, p, p, wgo, wdo, wo)


def _rope_tables(pos):
    half = ROPE_DIM // 2
    inv = ROPE_THETA ** (-jnp.arange(half, dtype=F32) * 2.0 / ROPE_DIM)
    ang = pos.astype(F32)[:, None] * inv[None, :]
    cos, sin = jnp.cos(ang), jnp.sin(ang)
    n = pos.shape[0]
    rest = DIFF_QK_HEAD - ROPE_DIM
    zeros = jnp.zeros((n, half), F32)
    pad = lambda parts, fill: jnp.tile(
        jnp.concatenate(parts + [jnp.full((n, rest), fill, F32)], axis=1), (1, LANES // DIFF_QK_HEAD))
    return pad([cos, cos], 1.0), pad([-sin, zeros], 0.0), pad([zeros, sin], 0.0)


def kernel(x_prompt, x_sample, cache_diff_k, cache_diff_v, state_gla, norm_ffn1, w_ffn1_gate, w_ffn1_up,
           w_ffn1_down, norm_mix, w_in, w_alpha2, b_alpha, q_norm, k_norm, diff_lambda, gla_norm, diff_norm,
           w_gla_o, w_diff_o, w_out, norm_ffn2, w_ffn2_gate, w_ffn2_up, w_ffn2_down):
    nb, seq, d = x_prompt.shape
    ndb, dseq, _ = x_sample.shape
    depth = w_in.shape[0]
    past = cache_diff_k.shape[2]
    assert seq % CHUNK == 0 and dseq == CHUNK and d == GLA_DV

    bf = lambda w: w.astype(BF16)
    wg1, wu1, wd1 = bf(w_ffn1_gate), bf(w_ffn1_up), bf(w_ffn1_down)
    wg2, wu2, wd2 = bf(w_ffn2_gate), bf(w_ffn2_up), bf(w_ffn2_down)
    a0 = P_QD
    qd0 = a0 + GLA_RANK
    vd0 = qd0 + 2 * DIFF_QK
    zg0 = vd0 + DIFF_V
    w_main = jnp.concatenate([w_in[:, :, :a0], w_in[:, :, qd0:vd0], w_in[:, :, zg0:], w_in[:, :, vd0:zg0]],
                             axis=-1).astype(BF16)
    w_a = jnp.pad(w_in[:, :, a0:a0 + GLA_RANK], ((0, 0), (0, 0), (0, LANES - GLA_RANK))).astype(BF16)
    w_2 = jnp.pad(w_alpha2, ((0, 0), (0, LANES - GLA_RANK), (0, 0))).astype(BF16)
    b_a = b_alpha.reshape(depth, 1, GLA_DK)
    wgo, wdo, wo = bf(w_gla_o), bf(w_diff_o), bf(w_out)
    n_groups = DIFF_QK // DIFF_QK_HEAD
    lane_group = (jnp.arange(2 * LANES) % LANES) // DIFF_QK_HEAD
    group_ones = (lane_group[:, None] == lane_group[None, :LANES]).astype(BF16)
    qn = jnp.tile(q_norm, (1, n_groups)).reshape(depth, 1, DIFF_QK)
    kn = jnp.tile(k_norm, (1, n_groups)).reshape(depth, 1, DIFF_QK)
    dn = diff_norm.reshape(depth, DIFF_HEADS, 1, DIFF_V_HEAD)

    streams = []
    for x0, n_seqs, slen, start in ((x_prompt, nb, seq, 0), (x_sample, ndb, dseq, past)):
        rows = n_seqs * slen
        tm = _pick_tile(rows, (1024, 512, 256, 128, 64))
        period = max(slen, tm)
        assert period % slen == 0 and period % tm == 0
        tables = _rope_tables(start + jnp.arange(period) % slen)
        streams.append(dict(x=x0.reshape(rows, d), slen=slen, tables=tables, k=None, v=None, s=None))

    for l in range(depth):
        lambda_init = 0.8 - 0.6 * math.exp(-0.3 * l)
        li = jnp.full((1,), lambda_init, F32)
        for si, st in enumerate(streams):
            slen = st["slen"]
            x, h = _ffn(st["x"], norm_ffn1[l][None], wg1, wu1, wd1, l, gain_next=norm_mix[l][None])
            p = _proj(h, w_main, l, 0, P_COLS, 2048)
            st["v"] = _proj(h, w_main, l, W_VD, DIFF_V, DIFF_V, stacked=(depth, st["v"]))
            qd = _qk_rope(p, P_QD, qn[l], group_ones, st["tables"], BF16)
            st["k"] = _qk_rope(p, P_KD, kn[l], group_ones, st["tables"], F32, layer=l, stacked=(depth, st["k"]))
            ag, st["s"] = _gla(p, h, w_a, w_2, b_a, gla_norm, l, seq_len=slen, states_prev=st["s"],
                               depth=depth, s0=state_gla if si == 1 else None)
            if si == 0:
                ad = _attn_prompt(qd, st["k"], st["v"], li, diff_lambda, dn, l, seq_len=slen)
            else:
                ad = _attn_sample(qd, st["k"], st["v"], cache_diff_k, cache_diff_v, li, diff_lambda, dn, l,
                                  seq_len=slen)
            x = _mix_out(x, ag, ad, p, wgo, wdo, wo, l)
            st["x"] = _ffn(x, norm_ffn2[l][None], wg2, wu2, wd2, l)

    pr, sa = streams
    head_k = (DIFF_HEADS, 2 * DIFF_QK_HEAD)
    head_v = (DIFF_HEADS, DIFF_V_HEAD)
    return (pr["x"].reshape(nb, seq, d), sa["x"].reshape(ndb, dseq, d),
            pr["k"].reshape(depth, nb, seq, *head_k), pr["v"].reshape(depth, nb, seq, *head_v), pr["s"],
            sa["k"].reshape(depth, ndb, dseq, *head_k), sa["v"].reshape(depth, ndb, dseq, *head_v), sa["s"])
```

```python
import functools
import math

import jax
import jax.numpy as jnp
from jax import lax
from jax.experimental import pallas as pl
from jax.experimental.pallas import tpu as pltpu

F32 = jnp.float32
BF16 = jnp.bfloat16

EPS = 1e-6
CHUNK = 64
GLA_HEADS = 4
GLA_DK_HEAD = 256
GLA_DV_HEAD = 512
GLA_DK = GLA_HEADS * GLA_DK_HEAD
GLA_DV = GLA_HEADS * GLA_DV_HEAD
GLA_RANK = 16
GLA_TAU = 16.0
DIFF_HEADS = 8
DIFF_QK_HEAD = 64
DIFF_V_HEAD = 128
DIFF_QK = DIFF_HEADS * 2 * DIFF_QK_HEAD
DIFF_V = DIFF_HEADS * DIFF_V_HEAD
ROPE_DIM = 16
ROPE_THETA = 500000.0
FFN_RES = 0.5
ATTN_HEADS_PER_STEP = 4

LANES = 128
VMEM_BYTES = 64 * 1024 * 1024
VMEM_LIMIT_BYTES = VMEM_BYTES - 2 * 1024 * 1024

P_QG = 0
P_KG = P_QG + GLA_DK
P_VG = P_KG + GLA_DK
P_RG = P_VG + GLA_DV
P_QD = P_RG + GLA_DV
P_KD = P_QD + DIFF_QK
P_ZG = P_KD + DIFF_QK
P_ZD = P_ZG + GLA_DV
P_COLS = P_ZD + GLA_DV
W_VD = P_COLS


def _pick_tile(n, candidates):
    for c in candidates:
        if n % c == 0:
            return c
    raise ValueError(f"no tile in {candidates} divides {n}")


def _params(*semantics):
    return pltpu.CompilerParams(dimension_semantics=semantics, vmem_limit_bytes=VMEM_LIMIT_BYTES)


def _nt_dot(a, b):
    return lax.dot_general(a, b, (((1,), (1,)), ((), ())), preferred_element_type=F32)


def _tn_dot(a, b):
    return lax.dot_general(a, b, (((0,), (0,)), ((), ())), preferred_element_type=F32)


def _dot(a, b):
    return jnp.dot(a, b, preferred_element_type=F32)


def _rms(x):
    return x * lax.rsqrt(jnp.mean(x * x, axis=-1, keepdims=True) + EPS)


def _split2(x):
    hi = x.astype(BF16)
    lo = (x - hi.astype(F32)).astype(BF16)
    return hi, lo


def _stacked_out(prev, n_in):
    if prev is None:
        return [], [], {}
    return [prev], [pl.BlockSpec(memory_space=pl.ANY)], {n_in: 0}


def _ffn_body(*refs, tf, f_valid, dn, emit_h):
    if emit_h:
        x_ref, gain_ref, wg_ref, wu_ref, wd_ref, gain2_ref, o_ref, h2_ref, h_ref = refs
    else:
        x_ref, gain_ref, wg_ref, wu_ref, wd_ref, o_ref, h_ref = refs
    j = pl.program_id(1)
    last = pl.num_programs(1) - 1
    d = o_ref.shape[1]

    def step(width, where):
        h = h_ref[...]
        g = _dot(h, wg_ref[:, :width])
        u = _dot(h, wu_ref[:, :width])
        a = (g * jax.nn.sigmoid(g) * u).astype(BF16)
        for n0 in range(0, d, dn):
            cols = slice(n0, n0 + dn)
            y = _dot(a, wd_ref[:width, cols])
            if where == "first":
                o_ref[:, cols] = y
            elif where == "middle":
                o_ref[:, cols] += y
            else:
                o_ref[:, cols] = x_ref[:, cols] + FFN_RES * (o_ref[:, cols] + y)

    @pl.when(j == 0)
    def _():
        h_ref[...] = (_rms(x_ref[...]) * gain_ref[...]).astype(BF16)
        step(tf, "first")

    @pl.when(jnp.logical_and(j > 0, j < last))
    def _():
        step(tf, "middle")

    @pl.when(j == last)
    def _():
        step(f_valid, "last")
        if emit_h:
            h2_ref[...] = (_rms(o_ref[...]) * gain2_ref[...]).astype(BF16)


def _ffn(x, gain, wg, wu, wd, layer, gain_next=None):
    t, d = x.shape
    f = wg.shape[2]
    tm = _pick_tile(t, (1024, 512, 256, 128, 64))
    emit_h = gain_next is not None
    tf = 512
    nf = pl.cdiv(f, tf)
    f_valid = f - (nf - 1) * tf
    assert nf >= 2 and f_valid % LANES == 0
    rows = pl.BlockSpec((tm, d), lambda i, j: (i, 0))
    vec = pl.BlockSpec((1, d), lambda i, j: (0, 0))
    in_specs = [
        rows, vec,
        pl.BlockSpec((None, d, tf), lambda i, j: (layer, 0, j)),
        pl.BlockSpec((None, d, tf), lambda i, j: (layer, 0, j)),
        pl.BlockSpec((None, tf, d), lambda i, j: (layer, j, 0)),
    ]
    inputs = [x, gain, wg, wu, wd]
    out_specs = [rows]
    out_shape = [jax.ShapeDtypeStruct((t, d), F32)]
    if emit_h:
        in_specs.append(vec)
        inputs.append(gain_next)
        out_specs.append(rows)
        out_shape.append(jax.ShapeDtypeStruct((t, d), BF16))
    out = pl.pallas_call(
        functools.partial(_ffn_body, tf=tf, f_valid=f_valid, dn=512, emit_h=emit_h),
        grid=(t // tm, nf),
        in_specs=in_specs,
        out_specs=out_specs,
        out_shape=out_shape,
        scratch_shapes=[pltpu.VMEM((tm, d), BF16)],
        compiler_params=_params("parallel", "arbitrary"),
        name="ffn",
    )(*inputs)
    return out if emit_h else out[0]


def _proj_body(h_ref, w_ref, *rest):
    rest[-1][...] = _dot(h_ref[...], w_ref[...])


def _proj(h, w, layer, col0, n_cols, tn, stacked=None):
    t, d = h.shape
    tm = _pick_tile(t, (1024, 512, 256, 128, 64))
    assert col0 % tn == 0 and n_cols % tn == 0
    j0 = col0 // tn
    in_specs = [
        pl.BlockSpec((tm, d), lambda j, i: (i, 0)),
        pl.BlockSpec((None, d, tn), lambda j, i: (layer, 0, j0 + j)),
    ]
    if stacked is None:
        out_spec = pl.BlockSpec((tm, tn), lambda j, i: (i, j))
        out_shape = jax.ShapeDtypeStruct((t, n_cols), F32)
        extra_in, extra_specs, aliases = [], [], {}
    else:
        depth, prev = stacked
        out_spec = pl.BlockSpec((None, tm, tn), lambda j, i: (layer, i, j))
        out_shape = jax.ShapeDtypeStruct((depth, t, n_cols), F32)
        extra_in, extra_specs, aliases = _stacked_out(prev, 2)
    return pl.pallas_call(
        _proj_body,
        grid=(n_cols // tn, t // tm),
        in_specs=in_specs + extra_specs,
        out_specs=out_spec,
        out_shape=out_shape,
        input_output_aliases=aliases,
        compiler_params=_params("parallel", "parallel"),
        name="proj",
    )(h, w, *extra_in)


def _qk_rope_body(x_ref, gain_ref, ones_ref, cos_ref, sl_ref, sh_ref, *rest):
    o_ref = rest[-1]
    half = ROPE_DIM // 2
    cos, sin_lo, sin_hi = cos_ref[...], sl_ref[...], sh_ref[...]
    for c0 in range(0, x_ref.shape[1], LANES):
        x = x_ref[:, c0:c0 + LANES]
        s_hi, s_lo = _split2(x * x)
        ssum = _dot(jnp.concatenate([s_hi, s_lo], axis=1), ones_ref[...])
        y = x * lax.rsqrt(ssum * (1.0 / DIFF_QK_HEAD) + EPS) * gain_ref[:, c0:c0 + LANES]
        out = y * cos + pltpu.roll(y, LANES - half, 1) * sin_lo + pltpu.roll(y, half, 1) * sin_hi
        o_ref[:, c0:c0 + LANES] = out.astype(o_ref.dtype)


def _qk_rope(p, col, gain, group_ones, tables, out_dtype, layer=0, stacked=None):
    t = p.shape[0]
    tm = _pick_tile(t, (1024, 512, 256, 128, 64))
    cos, sin_lo, sin_hi = tables
    period = cos.shape[0] // tm
    const = lambda a: pl.BlockSpec(a.shape, lambda i: (0, 0))
    table = pl.BlockSpec((tm, LANES), lambda i: (i % period, 0))
    in_specs = [pl.BlockSpec((tm, DIFF_QK), lambda i: (i, col // DIFF_QK)), const(gain), const(group_ones),
                table, table, table]
    if stacked is None:
        out_spec = pl.BlockSpec((tm, DIFF_QK), lambda i: (i, 0))
        out_shape = jax.ShapeDtypeStruct((t, DIFF_QK), out_dtype)
        extra_in, extra_specs, aliases = [], [], {}
    else:
        depth, prev = stacked
        out_spec = pl.BlockSpec((None, tm, DIFF_QK), lambda i: (layer, i, 0))
        out_shape = jax.ShapeDtypeStruct((depth, t, DIFF_QK), out_dtype)
        extra_in, extra_specs, aliases = _stacked_out(prev, len(in_specs))
    return pl.pallas_call(
        _qk_rope_body,
        grid=(t // tm,),
        in_specs=in_specs + extra_specs,
        out_specs=out_spec,
        out_shape=out_shape,
        input_output_aliases=aliases,
        compiler_params=_params("parallel"),
        name="qk_rope",
    )(p, gain, group_ones, cos, sin_lo, sin_hi, *extra_in)


def _log_sigmoid(x):
    return jnp.minimum(x, 0.0) - jnp.log1p(jnp.exp(-jnp.abs(x)))


def _gla_body(*refs, n_group, has_s0):
    q_ref, k_ref, v_ref, r_ref, h_ref, wa_ref, w2_ref, ba_ref, gn_ref = refs[:9]
    s0_ref = refs[9] if has_s0 else None
    o_ref, sout_ref, st_ref = refs[-3:]
    c = pl.program_id(1)
    seqs = range(n_group)
    pairs = [(s, hd) for s in seqs for hd in range(GLA_HEADS)]
    ks = lambda hd: slice(hd * GLA_DK_HEAD, (hd + 1) * GLA_DK_HEAD)
    vs = lambda hd: slice(hd * GLA_DV_HEAD, (hd + 1) * GLA_DV_HEAD)

    @pl.when(c == 0)
    def _():
        if has_s0:
            for s, hd in pairs:
                st_ref[s, hd] = s0_ref[s, hd].T
        else:
            st_ref[...] = jnp.zeros_like(st_ref)

    row = lax.broadcasted_iota(jnp.int32, (CHUNK, CHUNK), 0)
    col = lax.broadcasted_iota(jnp.int32, (CHUNK, CHUNK), 1)
    causal = row >= col
    tri = jnp.where(causal, 1.0, 0.0).astype(BF16)

    a = [_dot(h_ref[s], wa_ref[...]).astype(BF16) for s in seqs]
    z = [_dot(a[s], w2_ref[...]) for s in seqs]
    g = [_log_sigmoid(z[s] + ba_ref[...]) * (1.0 / GLA_TAU) for s in seqs]
    g_hi = [g[s].astype(BF16) for s in seqs]
    g_r1 = [g[s] - g_hi[s].astype(F32) for s in seqs]
    g_mid = [g_r1[s].astype(BF16) for s in seqs]
    g_lo = [(g_r1[s] - g_mid[s].astype(F32)).astype(BF16) for s in seqs]
    parts = [[_dot(tri, part[s]) for s in seqs] for part in (g_hi, g_mid, g_lo)]
    b = [parts[0][s] + parts[1][s] + parts[2][s] for s in seqs]
    b_mid = [b[s][CHUNK // 2:CHUNK // 2 + 1] for s in seqs]
    b_end = [b[s][CHUNK - 1:CHUNK] for s in seqs]
    q = [q_ref[s] * (GLA_DK_HEAD ** -0.5) for s in seqs]
    q_in = [(q[s] * jnp.exp(b[s])).astype(BF16) for s in seqs]
    q_rel = [(q[s] * jnp.exp(b[s] - b_mid[s])).astype(BF16) for s in seqs]
    k_rel = [(k_ref[s] * jnp.exp(b_mid[s] - b[s])).astype(BF16) for s in seqs]
    k_end = [(k_ref[s] * jnp.exp(b_end[s] - b[s])).astype(BF16) for s in seqs]
    decay = [jnp.exp(b_end[s]) for s in seqs]

    vb = {(s, hd): v_ref[s, :, vs(hd)].astype(BF16) for s, hd in pairs}
    att = {(s, hd): _nt_dot(q_rel[s][:, ks(hd)], k_rel[s][:, ks(hd)]) for s, hd in pairs}
    st = {(s, hd): st_ref[s, hd] for s, hd in pairs}
    o = {(s, hd): _nt_dot(q_in[s][:, ks(hd)], st[s, hd].astype(BF16)) for s, hd in pairs}
    upd = {(s, hd): _tn_dot(vb[s, hd], k_end[s][:, ks(hd)]) for s, hd in pairs}
    att = {(s, hd): jnp.where(causal, att[s, hd], 0.0).astype(BF16) for s, hd in pairs}
    o = {(s, hd): o[s, hd] + _dot(att[s, hd], vb[s, hd]) for s, hd in pairs}
    for s, hd in pairs:
        st_ref[s, hd] = decay[s][:, ks(hd)] * st[s, hd] + upd[s, hd]
    for s, hd in pairs:
        r = r_ref[s, :, vs(hd)]
        y = _rms(o[s, hd]) * gn_ref[hd:hd + 1, :] * (r * jax.nn.sigmoid(r))
        o_ref[s, :, vs(hd)] = y.astype(BF16)

    @pl.when(c == pl.num_programs(1) - 1)
    def _():
        for s, hd in pairs:
            sout_ref[s, hd] = st_ref[s, hd].T


def _gla(p, h, wa, w2, ba, gn, layer, *, seq_len, states_prev, depth, s0=None):
    t, d = h.shape
    cps = seq_len // CHUNK
    n_seqs = t // seq_len
    n_group = _pick_tile(n_seqs, (4, 2, 1)) if s0 is None else _pick_tile(n_seqs, (2, 1))
    p3 = p.reshape(n_seqs, seq_len, p.shape[1])
    h3 = h.reshape(n_seqs, seq_len, d)
    state_block = (None, n_group, GLA_HEADS, GLA_DK_HEAD, GLA_DV_HEAD)
    state_spec = pl.BlockSpec(state_block, lambda g, c: (layer, g, 0, 0, 0))
    cols = lambda width, col: pl.BlockSpec((n_group, CHUNK, width), lambda g, c: (g, c, col // width))
    in_specs = [
        cols(GLA_DK, P_QG), cols(GLA_DK, P_KG), cols(GLA_DV, P_VG), cols(GLA_DV, P_RG), cols(d, 0),
        pl.BlockSpec((None, d, LANES), lambda g, c: (layer, 0, 0)),
        pl.BlockSpec((None, LANES, GLA_DK), lambda g, c: (layer, 0, 0)),
        pl.BlockSpec((None, 1, GLA_DK), lambda g, c: (layer, 0, 0)),
        pl.BlockSpec((None, GLA_HEADS, GLA_DV_HEAD), lambda g, c: (layer, 0, 0)),
    ]
    inputs = [p3, p3, p3, p3, h3, wa, w2, ba, gn]
    if s0 is not None:
        in_specs.append(state_spec)
        inputs.append(s0)
    extra_in, extra_specs, aliases = _stacked_out(states_prev, len(inputs))
    aliases = {k: 1 for k in aliases}
    out, states = pl.pallas_call(
        functools.partial(_gla_body, n_group=n_group, has_s0=s0 is not None),
        grid=(n_seqs // n_group, cps),
        in_specs=in_specs + extra_specs,
        out_specs=[cols(GLA_DV, 0), state_spec],
        out_shape=[
            jax.ShapeDtypeStruct((n_seqs, seq_len, GLA_DV), BF16),
            jax.ShapeDtypeStruct((depth, n_seqs, GLA_HEADS, GLA_DK_HEAD, GLA_DV_HEAD), F32),
        ],
        input_output_aliases=aliases,
        scratch_shapes=[pltpu.VMEM((n_group, GLA_HEADS, GLA_DV_HEAD, GLA_DK_HEAD), F32)],
        compiler_params=_params("parallel", "arbitrary"),
        name="gla",
    )(*inputs, *extra_in)
    return out.reshape(t, GLA_DV), states


def _lambda_value(lam_ref, lambda_init):
    lp = lam_ref[...]
    d1 = jnp.sum(lp[0:1] * lp[1:2], axis=-1, keepdims=True)
    d2 = jnp.sum(lp[2:3] * lp[3:4], axis=-1, keepdims=True)
    return jnp.exp(d1) - jnp.exp(d2) + lambda_init


def _stack_maps(q):
    q = q.astype(F32) * (DIFF_QK_HEAD ** -0.5)
    lane = lax.broadcasted_iota(jnp.int32, q.shape, 1)
    first = lane < DIFF_QK_HEAD
    return jnp.concatenate([jnp.where(first, q, 0.0), jnp.where(first, 0.0, q)], axis=0).astype(BF16)


def _with_ones_column(v):
    lane = lax.broadcasted_iota(jnp.int32, v.shape, 1)
    return jnp.concatenate([v.astype(BF16), jnp.where(lane == 0, 1.0, 0.0).astype(BF16)], axis=1)


def _diff_attend(scores, values, lam):
    rows = scores[0].shape[0] // 2
    mx = functools.reduce(jnp.maximum, [jnp.max(s, axis=-1, keepdims=True) for s in scores])
    acc = functools.reduce(jnp.add, [_dot(jnp.exp(s - mx).astype(BF16), v) for s, v in zip(scores, values)])
    o = acc[:, :DIFF_V_HEAD] / acc[:, DIFF_V_HEAD:DIFF_V_HEAD + 1]
    return o[:rows] - lam * o[rows:]


def _attn_prompt_body(li_ref, q_ref, k_ref, v_ref, lam_ref, dn_ref, o_ref, *, tq, n_heads):
    seq_len = q_ref.shape[0]
    dv = DIFF_V_HEAD
    lambda_init = li_ref[0]
    lam = _lambda_value(lam_ref, lambda_init)
    row = lax.broadcasted_iota(jnp.int32, (2 * tq, tq), 0) % tq // CHUNK
    col = lax.broadcasted_iota(jnp.int32, (2 * tq, tq), 1) // CHUNK
    visible = row >= col
    n_tiles = seq_len // tq
    cols = lambda h: slice(h * dv, (h + 1) * dv)
    kb = [k_ref[:, cols(h)].astype(BF16) for h in range(n_heads)]
    vb = [_with_ones_column(v_ref[:, cols(h)]) for h in range(n_heads)]

    def tile_scores(h, i):
        lo, hi = i * tq, (i + 1) * tq
        qz = _stack_maps(q_ref[lo:hi, cols(h)])
        scores = [jnp.where(visible, _nt_dot(qz, kb[h][lo:hi]), -jnp.inf)]
        if i > 0:
            scores.append(_nt_dot(qz, kb[h][:lo]))
        return scores

    work = [(h, i) for h in range(n_heads) for i in range(n_tiles)]
    scores = tile_scores(*work[0])
    for n, (h, i) in enumerate(work):
        lo, hi = i * tq, (i + 1) * tq
        ahead = tile_scores(*work[n + 1]) if n + 1 < len(work) else None
        values = [vb[h][lo:hi]] + ([vb[h][:lo]] if i > 0 else [])
        o = _diff_attend(scores, values, lam)
        o_ref[lo:hi, cols(h)] = (_rms(o) * dn_ref[h] * (1.0 - lambda_init)).astype(BF16)
        scores = ahead


def _attn_prompt(qd, k_all, v_all, li, lam, dn, layer, *, seq_len):
    t = qd.shape[0]
    n_seqs = t // seq_len
    tq = _pick_tile(seq_len, (256, 128, 64))
    n_heads = ATTN_HEADS_PER_STEP
    width = n_heads * DIFF_V_HEAD
    kv = pl.BlockSpec((None, seq_len, width), lambda b, h: (layer, b, h))
    return pl.pallas_call(
        functools.partial(_attn_prompt_body, tq=tq, n_heads=n_heads),
        grid=(n_seqs, DIFF_HEADS // n_heads),
        in_specs=[
            pl.BlockSpec(memory_space=pltpu.SMEM),
            pl.BlockSpec((seq_len, width), lambda b, h: (b, h)),
            kv, kv,
            pl.BlockSpec((None, 4, DIFF_QK_HEAD), lambda b, h: (layer, 0, 0)),
            pl.BlockSpec((None, n_heads, 1, DIFF_V_HEAD), lambda b, h: (layer, h, 0, 0)),
        ],
        out_specs=pl.BlockSpec((seq_len, width), lambda b, h: (b, h)),
        out_shape=jax.ShapeDtypeStruct((t, DIFF_V), BF16),
        compiler_params=_params("parallel", "parallel"),
        name="attn_prompt",
    )(li, qd, k_all, v_all, lam, dn)


def _attn_sample_body(li_ref, q_ref, kc_ref, vc_ref, kn_ref, vn_ref, lam_ref, dn_ref, o_ref):
    past = kc_ref.shape[0] // DIFF_HEADS
    lambda_init = li_ref[0]
    lam = _lambda_value(lam_ref, lambda_init)
    cols = lambda h: slice(h * DIFF_V_HEAD, (h + 1) * DIFF_V_HEAD)
    head_rows = lambda h: pl.ds(h, past, stride=DIFF_HEADS)

    def head_scores(h):
        qz = _stack_maps(q_ref[:, cols(h)])
        return [_nt_dot(qz, kc_ref[head_rows(h), :].astype(BF16)), _nt_dot(qz, kn_ref[:, cols(h)].astype(BF16))]

    scores = head_scores(0)
    for h in range(DIFF_HEADS):
        ahead = head_scores(h + 1) if h + 1 < DIFF_HEADS else None
        values = [_with_ones_column(vc_ref[head_rows(h), :]), _with_ones_column(vn_ref[:, cols(h)])]
        o = _diff_attend(scores, values, lam)
        o_ref[:, cols(h)] = (_rms(o) * dn_ref[h] * (1.0 - lambda_init)).astype(BF16)
        scores = ahead


def _attn_sample(qd, k_all, v_all, cache_k, cache_v, li, lam, dn, layer, *, seq_len):
    t = qd.shape[0]
    n_seqs = t // seq_len
    depth, _, past, heads, dv = cache_k.shape
    rows = past * heads
    new = pl.BlockSpec((None, seq_len, heads * dv), lambda b: (layer, b, 0))
    old = pl.BlockSpec((None, None, rows, dv), lambda b: (layer, b, 0, 0))
    return pl.pallas_call(
        _attn_sample_body,
        grid=(n_seqs,),
        in_specs=[
            pl.BlockSpec(memory_space=pltpu.SMEM),
            pl.BlockSpec((seq_len, heads * dv), lambda b: (b, 0)),
            old, old, new, new,
            pl.BlockSpec((None, 4, DIFF_QK_HEAD), lambda b: (layer, 0, 0)),
            pl.BlockSpec((None, heads, 1, dv), lambda b: (layer, 0, 0, 0)),
        ],
        out_specs=pl.BlockSpec((seq_len, heads * dv), lambda b: (b, 0)),
        out_shape=jax.ShapeDtypeStruct((t, DIFF_V), BF16),
        compiler_params=_params("parallel"),
        name="attn_sample",
    )(li, qd, cache_k.reshape(depth, n_seqs, rows, dv), cache_v.reshape(depth, n_seqs, rows, dv),
      k_all, v_all, lam, dn)


def _mix_out_body(x_ref, ag_ref, ad_ref, zg_ref, zd_ref, wgo_ref, wdo_ref, wo_ref, o_ref):
    yg = _dot(ag_ref[...], wgo_ref[...])
    yd = _dot(ad_ref[...], wdo_ref[...])
    m = jax.nn.sigmoid(zg_ref[...]) * yg + jax.nn.sigmoid(zd_ref[...]) * yd
    o_ref[...] = x_ref[...] + _dot(m.astype(BF16), wo_ref[...])


def _mix_out(x, ag, ad, p, wgo, wdo, wo, layer):
    t, d = x.shape
    tm = _pick_tile(t, (256, 128, 64))
    rows = lambda width, col=0: pl.BlockSpec((tm, width), lambda i: (i, col // width))
    resident = lambda w: pl.BlockSpec((None,) + w.shape[1:], lambda i: (layer, 0, 0),
                                      pipeline_mode=pl.Buffered(1))
    return pl.pallas_call(
        _mix_out_body,
        grid=(t // tm,),
        in_specs=[
            rows(d), rows(GLA_DV), rows(DIFF_V), rows(d, P_ZG), rows(d, P_ZD),
            resident(wgo), resident(wdo), resident(wo),
        ],
        out_specs=rows(d),
        out_shape=jax.ShapeDtypeStruct((t, d), F32),
        compiler_params=_params("parallel"),
        name="mix_out",
    )(x, ag, ad, p, p, wgo, wdo, wo)


def _rope_tables(pos):
    half = ROPE_DIM // 2
    inv = ROPE_THETA ** (-jnp.arange(half, dtype=F32) * 2.0 / ROPE_DIM)
    ang = pos.astype(F32)[:, None] * inv[None, :]
    cos, sin = jnp.cos(ang), jnp.sin(ang)
    n = pos.shape[0]
    rest = DIFF_QK_HEAD - ROPE_DIM
    zeros = jnp.zeros((n, half), F32)
    pad = lambda parts, fill: jnp.tile(
        jnp.concatenate(parts + [jnp.full((n, rest), fill, F32)], axis=1), (1, LANES // DIFF_QK_HEAD))
    return pad([cos, cos], 1.0), pad([-sin, zeros], 0.0), pad([zeros, sin], 0.0)


def kernel(x_prompt, x_sample, cache_diff_k, cache_diff_v, state_gla, norm_ffn1, w_ffn1_gate, w_ffn1_up,
           w_ffn1_down, norm_mix, w_in, w_alpha2, b_alpha, q_norm, k_norm, diff_lambda, gla_norm, diff_norm,
           w_gla_o, w_diff_o, w_out, norm_ffn2, w_ffn2_gate, w_ffn2_up, w_ffn2_down):
    nb, seq, d = x_prompt.shape
    ndb, dseq, _ = x_sample.shape
    depth = w_in.shape[0]
    past = cache_diff_k.shape[2]
    assert seq % CHUNK == 0 and dseq == CHUNK and d == GLA_DV

    bf = lambda w: w.astype(BF16)
    wg1, wu1, wd1 = bf(w_ffn1_gate), bf(w_ffn1_up), bf(w_ffn1_down)
    wg2, wu2, wd2 = bf(w_ffn2_gate), bf(w_ffn2_up), bf(w_ffn2_down)
    a0 = P_QD
    qd0 = a0 + GLA_RANK
    vd0 = qd0 + 2 * DIFF_QK
    zg0 = vd0 + DIFF_V
    w_main = jnp.concatenate([w_in[:, :, :a0], w_in[:, :, qd0:vd0], w_in[:, :, zg0:], w_in[:, :, vd0:zg0]],
                             axis=-1).astype(BF16)
    w_a = jnp.pad(w_in[:, :, a0:a0 + GLA_RANK], ((0, 0), (0, 0), (0, LANES - GLA_RANK))).astype(BF16)
    w_2 = jnp.pad(w_alpha2, ((0, 0), (0, LANES - GLA_RANK), (0, 0))).astype(BF16)
    b_a = b_alpha.reshape(depth, 1, GLA_DK)
    wgo, wdo, wo = bf(w_gla_o), bf(w_diff_o), bf(w_out)
    n_groups = DIFF_QK // DIFF_QK_HEAD
    lane_group = (jnp.arange(2 * LANES) % LANES) // DIFF_QK_HEAD
    group_ones = (lane_group[:, None] == lane_group[None, :LANES]).astype(BF16)
    qn = jnp.tile(q_norm, (1, n_groups)).reshape(depth, 1, DIFF_QK)
    kn = jnp.tile(k_norm, (1, n_groups)).reshape(depth, 1, DIFF_QK)
    dn = diff_norm.reshape(depth, DIFF_HEADS, 1, DIFF_V_HEAD)

    streams = []
    for x0, n_seqs, slen, start in ((x_prompt, nb, seq, 0), (x_sample, ndb, dseq, past)):
        rows = n_seqs * slen
        tm = _pick_tile(rows, (1024, 512, 256, 128, 64))
        period = max(slen, tm)
        assert period % slen == 0 and period % tm == 0
        tables = _rope_tables(start + jnp.arange(period) % slen)
        streams.append(dict(x=x0.reshape(rows, d), slen=slen, tables=tables, k=None, v=None, s=None))

    for l in range(depth):
        lambda_init = 0.8 - 0.6 * math.exp(-0.3 * l)
        li = jnp.full((1,), lambda_init, F32)
        for si, st in enumerate(streams):
            slen = st["slen"]
            x, h = _ffn(st["x"], norm_ffn1[l][None], wg1, wu1, wd1, l, gain_next=norm_mix[l][None])
            p = _proj(h, w_main, l, 0, P_COLS, 2048)
            st["v"] = _proj(h, w_main, l, W_VD, DIFF_V, DIFF_V, stacked=(depth, st["v"]))
            qd = _qk_rope(p, P_QD, qn[l], group_ones, st["tables"], BF16)
            st["k"] = _qk_rope(p, P_KD, kn[l], group_ones, st["tables"], F32, layer=l, stacked=(depth, st["k"]))
            ag, st["s"] = _gla(p, h, w_a, w_2, b_a, gla_norm, l, seq_len=slen, states_prev=st["s"],
                               depth=depth, s0=state_gla if si == 1 else None)
            if si == 0:
                ad = _attn_prompt(qd, st["k"], st["v"], li, diff_lambda, dn, l, seq_len=slen)
            else:
                ad = _attn_sample(qd, st["k"], st["v"], cache_diff_k, cache_diff_v, li, diff_lambda, dn, l,
                                  seq_len=slen)
            x = _mix_out(x, ag, ad, p, wgo, wdo, wo, l)
            st["x"] = _ffn(x, norm_ffn2[l][None], wg2, wu2, wd2, l)

    pr, sa = streams
    head_k = (DIFF_HEADS, 2 * DIFF_QK_HEAD)
    head_v = (DIFF_HEADS, DIFF_V_HEAD)
    return (pr["x"].reshape(nb, seq, d), sa["x"].reshape(ndb, dseq, d),
            pr["k"].reshape(depth, nb, seq, *head_k), pr["v"].reshape(depth, nb, seq, *head_v), pr["s"],
            sa["k"].reshape(depth, ndb, dseq, *head_k), sa["v"].reshape(depth, ndb, dseq, *head_v), sa["s"])
```

```python
import functools
import math

import jax
import jax.numpy as jnp
from jax import lax
from jax.experimental import pallas as pl
from jax.experimental.pallas import tpu as pltpu

F32 = jnp.float32
BF16 = jnp.bfloat16

EPS = 1e-6
CHUNK = 64
GLA_HEADS = 4
GLA_DK_HEAD = 256
GLA_DV_HEAD = 512
GLA_DK = GLA_HEADS * GLA_DK_HEAD
GLA_DV = GLA_HEADS * GLA_DV_HEAD
GLA_RANK = 16
GLA_TAU = 16.0
DIFF_HEADS = 8
DIFF_QK_HEAD = 64
DIFF_V_HEAD = 128
DIFF_QK = DIFF_HEADS * 2 * DIFF_QK_HEAD
DIFF_V = DIFF_HEADS * DIFF_V_HEAD
ROPE_DIM = 16
ROPE_THETA = 500000.0
FFN_RES = 0.5
ATTN_HEADS_PER_STEP = 4

LANES = 128
VMEM_BYTES = 64 * 1024 * 1024
VMEM_LIMIT_BYTES = VMEM_BYTES - 2 * 1024 * 1024

P_QG = 0
P_KG = P_QG + GLA_DK
P_VG = P_KG + GLA_DK
P_RG = P_VG + GLA_DV
P_QD = P_RG + GLA_DV
P_KD = P_QD + DIFF_QK
P_ZG = P_KD + DIFF_QK
P_ZD = P_ZG + GLA_DV
P_COLS = P_ZD + GLA_DV
W_VD = P_COLS
REGROUP_TN = 1024


def _pick_tile(n, candidates):
    for c in candidates:
        if n % c == 0:
            return c
    raise ValueError(f"no tile in {candidates} divides {n}")


def _params(*semantics):
    return pltpu.CompilerParams(dimension_semantics=semantics, vmem_limit_bytes=VMEM_LIMIT_BYTES)


def _nt_dot(a, b):
    return lax.dot_general(a, b, (((1,), (1,)), ((), ())), preferred_element_type=F32)


def _tn_dot(a, b):
    return lax.dot_general(a, b, (((0,), (0,)), ((), ())), preferred_element_type=F32)


def _dot(a, b):
    return jnp.dot(a, b, preferred_element_type=F32)


def _rms(x):
    return x * lax.rsqrt(jnp.mean(x * x, axis=-1, keepdims=True) + EPS)


def _split2(x):
    hi = x.astype(BF16)
    lo = (x - hi.astype(F32)).astype(BF16)
    return hi, lo


def _stacked_out(prev, n_in):
    if prev is None:
        return [], [], {}
    return [prev], [pl.BlockSpec(memory_space=pl.ANY)], {n_in: 0}


def _ffn_body(*refs, tf, f_valid, dn, emit_h):
    if emit_h:
        x_ref, gain_ref, wg_ref, wu_ref, wd_ref, gain2_ref, o_ref, h2_ref, h_ref = refs
    else:
        x_ref, gain_ref, wg_ref, wu_ref, wd_ref, o_ref, h_ref = refs
    j = pl.program_id(1)
    last = pl.num_programs(1) - 1
    d = o_ref.shape[1]

    def step(width, where):
        h = h_ref[...]
        g = _dot(h, wg_ref[:, :width])
        u = _dot(h, wu_ref[:, :width])
        a = (g * jax.nn.sigmoid(g) * u).astype(BF16)
        for n0 in range(0, d, dn):
            cols = slice(n0, n0 + dn)
            y = _dot(a, wd_ref[:width, cols])
            if where == "first":
                o_ref[:, cols] = y
            elif where == "middle":
                o_ref[:, cols] += y
            else:
                o_ref[:, cols] = x_ref[:, cols] + FFN_RES * (o_ref[:, cols] + y)

    @pl.when(j == 0)
    def _():
        h_ref[...] = (_rms(x_ref[...]) * gain_ref[...]).astype(BF16)
        step(tf, "first")

    @pl.when(jnp.logical_and(j > 0, j < last))
    def _():
        step(tf, "middle")

    @pl.when(j == last)
    def _():
        step(f_valid, "last")
        if emit_h:
            h2_ref[...] = (_rms(o_ref[...]) * gain2_ref[...]).astype(BF16)


def _ffn(x, gain, wg, wu, wd, layer, gain_next=None):
    t, d = x.shape
    f = wg.shape[2]
    tm = _pick_tile(t, (1024, 512, 256, 128, 64))
    emit_h = gain_next is not None
    tf = 512
    nf = pl.cdiv(f, tf)
    f_valid = f - (nf - 1) * tf
    assert nf >= 2 and f_valid % LANES == 0
    rows = pl.BlockSpec((tm, d), lambda i, j: (i, 0))
    vec = pl.BlockSpec((1, d), lambda i, j: (0, 0))
    in_specs = [
        rows, vec,
        pl.BlockSpec((None, d, tf), lambda i, j: (layer, 0, j)),
        pl.BlockSpec((None, d, tf), lambda i, j: (layer, 0, j)),
        pl.BlockSpec((None, tf, d), lambda i, j: (layer, j, 0)),
    ]
    inputs = [x, gain, wg, wu, wd]
    out_specs = [rows]
    out_shape = [jax.ShapeDtypeStruct((t, d), F32)]
    if emit_h:
        in_specs.append(vec)
        inputs.append(gain_next)
        out_specs.append(rows)
        out_shape.append(jax.ShapeDtypeStruct((t, d), BF16))
    out = pl.pallas_call(
        functools.partial(_ffn_body, tf=tf, f_valid=f_valid, dn=512, emit_h=emit_h),
        grid=(t // tm, nf),
        in_specs=in_specs,
        out_specs=out_specs,
        out_shape=out_shape,
        scratch_shapes=[pltpu.VMEM((tm, d), BF16)],
        compiler_params=_params("parallel", "arbitrary"),
        name="ffn",
    )(*inputs)
    return out if emit_h else out[0]


def _regroup_body(a_ref, b_ref, o_ref, *, shift, n_plain):
    j = pl.program_id(1)

    @pl.when(j < n_plain)
    def _():
        o_ref[...] = a_ref[...].astype(BF16)

    @pl.when(j >= n_plain)
    def _():
        wide = jnp.concatenate([a_ref[...], b_ref[...]], axis=1)
        o_ref[...] = wide[:, shift:shift + o_ref.shape[1]].astype(BF16)


def _regroup_w_in(w_in):
    depth, d, _ = w_in.shape
    tn = REGROUP_TN
    n_plain = P_QD // tn
    n_out = (P_COLS + DIFF_V) // tn
    vd_src = (P_QD + 2 * DIFF_QK) // tn
    n_qk = 2 * DIFF_QK // tn

    def src(j):
        after_qk = jnp.where(j < n_out - DIFF_V // tn, j + DIFF_V // tn, vd_src)
        return jnp.where(j < n_plain + n_qk, j, after_qk)

    lanes_per_tile = tn // LANES
    return pl.pallas_call(
        functools.partial(_regroup_body, shift=GLA_RANK, n_plain=n_plain),
        grid=(depth, n_out),
        in_specs=[
            pl.BlockSpec((None, d, tn), lambda l, j: (l, 0, src(j))),
            pl.BlockSpec((None, d, LANES), lambda l, j: (l, 0, (src(j) + 1) * lanes_per_tile)),
        ],
        out_specs=pl.BlockSpec((None, d, tn), lambda l, j: (l, 0, j)),
        out_shape=jax.ShapeDtypeStruct((depth, d, n_out * tn), BF16),
        compiler_params=_params("parallel", "parallel"),
        name="regroup_w_in",
    )(w_in, w_in)


def _proj_body(h_ref, w_ref, *rest):
    rest[-1][...] = _dot(h_ref[...], w_ref[...])


def _proj(h, w, layer, col0, n_cols, tn, stacked=None):
    t, d = h.shape
    tm = _pick_tile(t, (1024, 512, 256, 128, 64))
    assert col0 % tn == 0 and n_cols % tn == 0
    j0 = col0 // tn
    in_specs = [
        pl.BlockSpec((tm, d), lambda j, i: (i, 0)),
        pl.BlockSpec((None, d, tn), lambda j, i: (layer, 0, j0 + j)),
    ]
    if stacked is None:
        out_spec = pl.BlockSpec((tm, tn), lambda j, i: (i, j))
        out_shape = jax.ShapeDtypeStruct((t, n_cols), F32)
        extra_in, extra_specs, aliases = [], [], {}
    else:
        depth, prev = stacked
        out_spec = pl.BlockSpec((None, tm, tn), lambda j, i: (layer, i, j))
        out_shape = jax.ShapeDtypeStruct((depth, t, n_cols), F32)
        extra_in, extra_specs, aliases = _stacked_out(prev, 2)
    return pl.pallas_call(
        _proj_body,
        grid=(n_cols // tn, t // tm),
        in_specs=in_specs + extra_specs,
        out_specs=out_spec,
        out_shape=out_shape,
        input_output_aliases=aliases,
        compiler_params=_params("parallel", "parallel"),
        name="proj",
    )(h, w, *extra_in)


def _qk_rope_body(x_ref, gain_ref, ones_ref, cos_ref, sl_ref, sh_ref, *rest):
    o_ref = rest[-1]
    half = ROPE_DIM // 2
    cos, sin_lo, sin_hi = cos_ref[...], sl_ref[...], sh_ref[...]
    for c0 in range(0, x_ref.shape[1], LANES):
        x = x_ref[:, c0:c0 + LANES]
        s_hi, s_lo = _split2(x * x)
        ssum = _dot(jnp.concatenate([s_hi, s_lo], axis=1), ones_ref[...])
        y = x * lax.rsqrt(ssum * (1.0 / DIFF_QK_HEAD) + EPS) * gain_ref[:, c0:c0 + LANES]
        out = y * cos + pltpu.roll(y, LANES - half, 1) * sin_lo + pltpu.roll(y, half, 1) * sin_hi
        o_ref[:, c0:c0 + LANES] = out.astype(o_ref.dtype)


def _qk_rope(p, col, gain, group_ones, tables, out_dtype, layer=0, stacked=None):
    t = p.shape[0]
    tm = _pick_tile(t, (1024, 512, 256, 128, 64))
    cos, sin_lo, sin_hi = tables
    period = cos.shape[0] // tm
    const = lambda a: pl.BlockSpec(a.shape, lambda i: (0, 0))
    table = pl.BlockSpec((tm, LANES), lambda i: (i % period, 0))
    in_specs = [pl.BlockSpec((tm, DIFF_QK), lambda i: (i, col // DIFF_QK)), const(gain), const(group_ones),
                table, table, table]
    if stacked is None:
        out_spec = pl.BlockSpec((tm, DIFF_QK), lambda i: (i, 0))
        out_shape = jax.ShapeDtypeStruct((t, DIFF_QK), out_dtype)
        extra_in, extra_specs, aliases = [], [], {}
    else:
        depth, prev = stacked
        out_spec = pl.BlockSpec((None, tm, DIFF_QK), lambda i: (layer, i, 0))
        out_shape = jax.ShapeDtypeStruct((depth, t, DIFF_QK), out_dtype)
        extra_in, extra_specs, aliases = _stacked_out(prev, len(in_specs))
    return pl.pallas_call(
        _qk_rope_body,
        grid=(t // tm,),
        in_specs=in_specs + extra_specs,
        out_specs=out_spec,
        out_shape=out_shape,
        input_output_aliases=aliases,
        compiler_params=_params("parallel"),
        name="qk_rope",
    )(p, gain, group_ones, cos, sin_lo, sin_hi, *extra_in)


def _log_sigmoid(x):
    return jnp.minimum(x, 0.0) - jnp.log1p(jnp.exp(-jnp.abs(x)))


def _gla_body(*refs, n_group, has_s0):
    q_ref, k_ref, v_ref, r_ref, h_ref, wa_ref, w2_ref, ba_ref, gn_ref = refs[:9]
    s0_ref = refs[9] if has_s0 else None
    o_ref, sout_ref, st_ref = refs[-3:]
    c = pl.program_id(1)
    seqs = range(n_group)
    pairs = [(s, hd) for s in seqs for hd in range(GLA_HEADS)]
    ks = lambda hd: slice(hd * GLA_DK_HEAD, (hd + 1) * GLA_DK_HEAD)
    vs = lambda hd: slice(hd * GLA_DV_HEAD, (hd + 1) * GLA_DV_HEAD)

    @pl.when(c == 0)
    def _():
        if has_s0:
            for s, hd in pairs:
                st_ref[s, hd] = s0_ref[s, hd].T
        else:
            st_ref[...] = jnp.zeros_like(st_ref)

    row = lax.broadcasted_iota(jnp.int32, (CHUNK, CHUNK), 0)
    col = lax.broadcasted_iota(jnp.int32, (CHUNK, CHUNK), 1)
    causal = row >= col
    tri = jnp.where(causal, 1.0, 0.0).astype(BF16)

    a = [_dot(h_ref[s], wa_ref[...]).astype(BF16) for s in seqs]
    z = [_dot(a[s], w2_ref[...]) for s in seqs]
    g = [_log_sigmoid(z[s] + ba_ref[...]) * (1.0 / GLA_TAU) for s in seqs]
    g_hi = [g[s].astype(BF16) for s in seqs]
    g_r1 = [g[s] - g_hi[s].astype(F32) for s in seqs]
    g_mid = [g_r1[s].astype(BF16) for s in seqs]
    g_lo = [(g_r1[s] - g_mid[s].astype(F32)).astype(BF16) for s in seqs]
    parts = [[_dot(tri, part[s]) for s in seqs] for part in (g_hi, g_mid, g_lo)]
    b = [parts[0][s] + parts[1][s] + parts[2][s] for s in seqs]
    b_mid = [b[s][CHUNK // 2:CHUNK // 2 + 1] for s in seqs]
    b_end = [b[s][CHUNK - 1:CHUNK] for s in seqs]
    q = [q_ref[s] * (GLA_DK_HEAD ** -0.5) for s in seqs]
    q_in = [(q[s] * jnp.exp(b[s])).astype(BF16) for s in seqs]
    q_rel = [(q[s] * jnp.exp(b[s] - b_mid[s])).astype(BF16) for s in seqs]
    k_rel = [(k_ref[s] * jnp.exp(b_mid[s] - b[s])).astype(BF16) for s in seqs]
    k_end = [(k_ref[s] * jnp.exp(b_end[s] - b[s])).astype(BF16) for s in seqs]
    decay = [jnp.exp(b_end[s]) for s in seqs]

    vb = {(s, hd): v_ref[s, :, vs(hd)].astype(BF16) for s, hd in pairs}
    att = {(s, hd): _nt_dot(q_rel[s][:, ks(hd)], k_rel[s][:, ks(hd)]) for s, hd in pairs}
    st = {(s, hd): st_ref[s, hd] for s, hd in pairs}
    o = {(s, hd): _nt_dot(q_in[s][:, ks(hd)], st[s, hd].astype(BF16)) for s, hd in pairs}
    upd = {(s, hd): _tn_dot(vb[s, hd], k_end[s][:, ks(hd)]) for s, hd in pairs}
    att = {(s, hd): jnp.where(causal, att[s, hd], 0.0).astype(BF16) for s, hd in pairs}
    o = {(s, hd): o[s, hd] + _dot(att[s, hd], vb[s, hd]) for s, hd in pairs}
    for s, hd in pairs:
        st_ref[s, hd] = decay[s][:, ks(hd)] * st[s, hd] + upd[s, hd]
    for s, hd in pairs:
        r = r_ref[s, :, vs(hd)]
        y = _rms(o[s, hd]) * gn_ref[hd:hd + 1, :] * (r * jax.nn.sigmoid(r))
        o_ref[s, :, vs(hd)] = y.astype(BF16)

    @pl.when(c == pl.num_programs(1) - 1)
    def _():
        for s, hd in pairs:
            sout_ref[s, hd] = st_ref[s, hd].T


def _gla(p, h, wa, w2, ba, gn, layer, *, seq_len, states_prev, depth, s0=None):
    t, d = h.shape
    cps = seq_len // CHUNK
    n_seqs = t // seq_len
    n_group = _pick_tile(n_seqs, (4, 2, 1)) if s0 is None else _pick_tile(n_seqs, (2, 1))
    p3 = p.reshape(n_seqs, seq_len, p.shape[1])
    h3 = h.reshape(n_seqs, seq_len, d)
    state_block = (None, n_group, GLA_HEADS, GLA_DK_HEAD, GLA_DV_HEAD)
    state_spec = pl.BlockSpec(state_block, lambda g, c: (layer, g, 0, 0, 0))
    cols = lambda width, col: pl.BlockSpec((n_group, CHUNK, width), lambda g, c: (g, c, col // width))
    in_specs = [
        cols(GLA_DK, P_QG), cols(GLA_DK, P_KG), cols(GLA_DV, P_VG), cols(GLA_DV, P_RG), cols(d, 0),
        pl.BlockSpec((None, d, LANES), lambda g, c: (layer, 0, 0)),
        pl.BlockSpec((None, LANES, GLA_DK), lambda g, c: (layer, 0, 0)),
        pl.BlockSpec((None, 1, GLA_DK), lambda g, c: (layer, 0, 0)),
        pl.BlockSpec((None, GLA_HEADS, GLA_DV_HEAD), lambda g, c: (layer, 0, 0)),
    ]
    inputs = [p3, p3, p3, p3, h3, wa, w2, ba, gn]
    if s0 is not None:
        in_specs.append(state_spec)
        inputs.append(s0)
    extra_in, extra_specs, aliases = _stacked_out(states_prev, len(inputs))
    aliases = {k: 1 for k in aliases}
    out, states = pl.pallas_call(
        functools.partial(_gla_body, n_group=n_group, has_s0=s0 is not None),
        grid=(n_seqs // n_group, cps),
        in_specs=in_specs + extra_specs,
        out_specs=[cols(GLA_DV, 0), state_spec],
        out_shape=[
            jax.ShapeDtypeStruct((n_seqs, seq_len, GLA_DV), BF16),
            jax.ShapeDtypeStruct((depth, n_seqs, GLA_HEADS, GLA_DK_HEAD, GLA_DV_HEAD), F32),
        ],
        input_output_aliases=aliases,
        scratch_shapes=[pltpu.VMEM((n_group, GLA_HEADS, GLA_DV_HEAD, GLA_DK_HEAD), F32)],
        compiler_params=_params("parallel", "arbitrary"),
        name="gla",
    )(*inputs, *extra_in)
    return out.reshape(t, GLA_DV), states


def _lambda_value(lam_ref, lambda_init):
    lp = lam_ref[...]
    d1 = jnp.sum(lp[0:1] * lp[1:2], axis=-1, keepdims=True)
    d2 = jnp.sum(lp[2:3] * lp[3:4], axis=-1, keepdims=True)
    return jnp.exp(d1) - jnp.exp(d2) + lambda_init


def _stack_maps(q):
    q = q.astype(F32) * (DIFF_QK_HEAD ** -0.5)
    lane = lax.broadcasted_iota(jnp.int32, q.shape, 1)
    first = lane < DIFF_QK_HEAD
    return jnp.concatenate([jnp.where(first, q, 0.0), jnp.where(first, 0.0, q)], axis=0).astype(BF16)


def _with_ones_column(v):
    lane = lax.broadcasted_iota(jnp.int32, v.shape, 1)
    return jnp.concatenate([v.astype(BF16), jnp.where(lane == 0, 1.0, 0.0).astype(BF16)], axis=1)


def _diff_attend(scores, values, lam):
    rows = scores[0].shape[0] // 2
    mx = functools.reduce(jnp.maximum, [jnp.max(s, axis=-1, keepdims=True) for s in scores])
    acc = functools.reduce(jnp.add, [_dot(jnp.exp(s - mx).astype(BF16), v) for s, v in zip(scores, values)])
    o = acc[:, :DIFF_V_HEAD] / acc[:, DIFF_V_HEAD:DIFF_V_HEAD + 1]
    return o[:rows] - lam * o[rows:]


def _attn_prompt_body(li_ref, q_ref, k_ref, v_ref, lam_ref, dn_ref, o_ref, *, tq, n_heads):
    seq_len = q_ref.shape[0]
    dv = DIFF_V_HEAD
    lambda_init = li_ref[0]
    lam = _lambda_value(lam_ref, lambda_init)
    row = lax.broadcasted_iota(jnp.int32, (2 * tq, tq), 0) % tq // CHUNK
    col = lax.broadcasted_iota(jnp.int32, (2 * tq, tq), 1) // CHUNK
    visible = row >= col
    n_tiles = seq_len // tq
    cols = lambda h: slice(h * dv, (h + 1) * dv)
    kb = [k_ref[:, cols(h)].astype(BF16) for h in range(n_heads)]
    vb = [_with_ones_column(v_ref[:, cols(h)]) for h in range(n_heads)]

    def tile_scores(h, i):
        lo, hi = i * tq, (i + 1) * tq
        qz = _stack_maps(q_ref[lo:hi, cols(h)])
        scores = [jnp.where(visible, _nt_dot(qz, kb[h][lo:hi]), -jnp.inf)]
        if i > 0:
            scores.append(_nt_dot(qz, kb[h][:lo]))
        return scores

    work = [(h, i) for h in range(n_heads) for i in range(n_tiles)]
    scores = tile_scores(*work[0])
    for n, (h, i) in enumerate(work):
        lo, hi = i * tq, (i + 1) * tq
        ahead = tile_scores(*work[n + 1]) if n + 1 < len(work) else None
        values = [vb[h][lo:hi]] + ([vb[h][:lo]] if i > 0 else [])
        o = _diff_attend(scores, values, lam)
        o_ref[lo:hi, cols(h)] = (_rms(o) * dn_ref[h] * (1.0 - lambda_init)).astype(BF16)
        scores = ahead


def _attn_prompt(qd, k_all, v_all, li, lam, dn, layer, *, seq_len):
    t = qd.shape[0]
    n_seqs = t // seq_len
    tq = _pick_tile(seq_len, (256, 128, 64))
    n_heads = ATTN_HEADS_PER_STEP
    width = n_heads * DIFF_V_HEAD
    kv = pl.BlockSpec((None, seq_len, width), lambda b, h: (layer, b, h))
    return pl.pallas_call(
        functools.partial(_attn_prompt_body, tq=tq, n_heads=n_heads),
        grid=(n_seqs, DIFF_HEADS // n_heads),
        in_specs=[
            pl.BlockSpec(memory_space=pltpu.SMEM),
            pl.BlockSpec((seq_len, width), lambda b, h: (b, h)),
            kv, kv,
            pl.BlockSpec((None, 4, DIFF_QK_HEAD), lambda b, h: (layer, 0, 0)),
            pl.BlockSpec((None, n_heads, 1, DIFF_V_HEAD), lambda b, h: (layer, h, 0, 0)),
        ],
        out_specs=pl.BlockSpec((seq_len, width), lambda b, h: (b, h)),
        out_shape=jax.ShapeDtypeStruct((t, DIFF_V), BF16),
        compiler_params=_params("parallel", "parallel"),
        name="attn_prompt",
    )(li, qd, k_all, v_all, lam, dn)


def _attn_sample_body(li_ref, q_ref, kc_ref, vc_ref, kn_ref, vn_ref, lam_ref, dn_ref, o_ref):
    past = kc_ref.shape[0] // DIFF_HEADS
    lambda_init = li_ref[0]
    lam = _lambda_value(lam_ref, lambda_init)
    cols = lambda h: slice(h * DIFF_V_HEAD, (h + 1) * DIFF_V_HEAD)
    head_rows = lambda h: pl.ds(h, past, stride=DIFF_HEADS)

    def head_scores(h):
        qz = _stack_maps(q_ref[:, cols(h)])
        return [_nt_dot(qz, kc_ref[head_rows(h), :].astype(BF16)), _nt_dot(qz, kn_ref[:, cols(h)].astype(BF16))]

    scores = head_scores(0)
    for h in range(DIFF_HEADS):
        ahead = head_scores(h + 1) if h + 1 < DIFF_HEADS else None
        values = [_with_ones_column(vc_ref[head_rows(h), :]), _with_ones_column(vn_ref[:, cols(h)])]
        o = _diff_attend(scores, values, lam)
        o_ref[:, cols(h)] = (_rms(o) * dn_ref[h] * (1.0 - lambda_init)).astype(BF16)
        scores = ahead


def _attn_sample(qd, k_all, v_all, cache_k, cache_v, li, lam, dn, layer, *, seq_len):
    t = qd.shape[0]
    n_seqs = t // seq_len
    depth, _, past, heads, dv = cache_k.shape
    rows = past * heads
    new = pl.BlockSpec((None, seq_len, heads * dv), lambda b: (layer, b, 0))
    old = pl.BlockSpec((None, None, rows, dv), lambda b: (layer, b, 0, 0))
    return pl.pallas_call(
        _attn_sample_body,
        grid=(n_seqs,),
        in_specs=[
            pl.BlockSpec(memory_space=pltpu.SMEM),
            pl.BlockSpec((seq_len, heads * dv), lambda b: (b, 0)),
            old, old, new, new,
            pl.BlockSpec((None, 4, DIFF_QK_HEAD), lambda b: (layer, 0, 0)),
            pl.BlockSpec((None, heads, 1, dv), lambda b: (layer, 0, 0, 0)),
        ],
        out_specs=pl.BlockSpec((seq_len, heads * dv), lambda b: (b, 0)),
        out_shape=jax.ShapeDtypeStruct((t, DIFF_V), BF16),
        compiler_params=_params("parallel"),
        name="attn_sample",
    )(li, qd, cache_k.reshape(depth, n_seqs, rows, dv), cache_v.reshape(depth, n_seqs, rows, dv),
      k_all, v_all, lam, dn)


def _mix_out_body(x_ref, ag_ref, ad_ref, zg_ref, zd_ref, wgo_ref, wdo_ref, wo_ref, o_ref):
    yg = _dot(ag_ref[...], wgo_ref[...])
    yd = _dot(ad_ref[...], wdo_ref[...])
    m = jax.nn.sigmoid(zg_ref[...]) * yg + jax.nn.sigmoid(zd_ref[...]) * yd
    o_ref[...] = x_ref[...] + _dot(m.astype(BF16), wo_ref[...])


def _mix_out(x, ag, ad, p, wgo, wdo, wo, layer):
    t, d = x.shape
    tm = _pick_tile(t, (256, 128, 64))
    rows = lambda width, col=0: pl.BlockSpec((tm, width), lambda i: (i, col // width))
    resident = lambda w: pl.BlockSpec((None,) + w.shape[1:], lambda i: (layer, 0, 0),
                                      pipeline_mode=pl.Buffered(1))
    return pl.pallas_call(
        _mix_out_body,
        grid=(t // tm,),
        in_specs=[
            rows(d), rows(GLA_DV), rows(DIFF_V), rows(d, P_ZG), rows(d, P_ZD),
            resident(wgo), resident(wdo), resident(wo),
        ],
        out_specs=rows(d),
        out_shape=jax.ShapeDtypeStruct((t, d), F32),
        compiler_params=_params("parallel"),
        name="mix_out",
    )(x, ag, ad, p, p, wgo, wdo, wo)


def _rope_tables(pos):
    half = ROPE_DIM // 2
    inv = ROPE_THETA ** (-jnp.arange(half, dtype=F32) * 2.0 / ROPE_DIM)
    ang = pos.astype(F32)[:, None] * inv[None, :]
    cos, sin = jnp.cos(ang), jnp.sin(ang)
    n = pos.shape[0]
    rest = DIFF_QK_HEAD - ROPE_DIM
    zeros = jnp.zeros((n, half), F32)
    pad = lambda parts, fill: jnp.tile(
        jnp.concatenate(parts + [jnp.full((n, rest), fill, F32)], axis=1), (1, LANES // DIFF_QK_HEAD))
    return pad([cos, cos], 1.0), pad([-sin, zeros], 0.0), pad([zeros, sin], 0.0)


def kernel(x_prompt, x_sample, cache_diff_k, cache_diff_v, state_gla, norm_ffn1, w_ffn1_gate, w_ffn1_up,
           w_ffn1_down, norm_mix, w_in, w_alpha2, b_alpha, q_norm, k_norm, diff_lambda, gla_norm, diff_norm,
           w_gla_o, w_diff_o, w_out, norm_ffn2, w_ffn2_gate, w_ffn2_up, w_ffn2_down):
    nb, seq, d = x_prompt.shape
    ndb, dseq, _ = x_sample.shape
    depth = w_in.shape[0]
    past = cache_diff_k.shape[2]
    assert seq % CHUNK == 0 and dseq == CHUNK and d == GLA_DV

    bf = lambda w: w.astype(BF16)
    wg1, wu1, wd1 = bf(w_ffn1_gate), bf(w_ffn1_up), bf(w_ffn1_down)
    wg2, wu2, wd2 = bf(w_ffn2_gate), bf(w_ffn2_up), bf(w_ffn2_down)
    a0 = P_QD
    w_main = _regroup_w_in(w_in)
    w_a = jnp.pad(w_in[:, :, a0:a0 + GLA_RANK], ((0, 0), (0, 0), (0, LANES - GLA_RANK))).astype(BF16)
    w_2 = jnp.pad(w_alpha2, ((0, 0), (0, LANES - GLA_RANK), (0, 0))).astype(BF16)
    b_a = b_alpha.reshape(depth, 1, GLA_DK)
    wgo, wdo, wo = bf(w_gla_o), bf(w_diff_o), bf(w_out)
    n_groups = DIFF_QK // DIFF_QK_HEAD
    lane_group = (jnp.arange(2 * LANES) % LANES) // DIFF_QK_HEAD
    group_ones = (lane_group[:, None] == lane_group[None, :LANES]).astype(BF16)
    qn = jnp.tile(q_norm, (1, n_groups)).reshape(depth, 1, DIFF_QK)
    kn = jnp.tile(k_norm, (1, n_groups)).reshape(depth, 1, DIFF_QK)
    dn = diff_norm.reshape(depth, DIFF_HEADS, 1, DIFF_V_HEAD)

    streams = []
    for x0, n_seqs, slen, start in ((x_prompt, nb, seq, 0), (x_sample, ndb, dseq, past)):
        rows = n_seqs * slen
        tm = _pick_tile(rows, (1024, 512, 256, 128, 64))
        period = max(slen, tm)
        assert period % slen == 0 and period % tm == 0
        tables = _rope_tables(start + jnp.arange(period) % slen)
        streams.append(dict(x=x0.reshape(rows, d), slen=slen, tables=tables, k=None, v=None, s=None))

    for l in range(depth):
        lambda_init = 0.8 - 0.6 * math.exp(-0.3 * l)
        li = jnp.full((1,), lambda_init, F32)
        for si, st in enumerate(streams):
            slen = st["slen"]
            x, h = _ffn(st["x"], norm_ffn1[l][None], wg1, wu1, wd1, l, gain_next=norm_mix[l][None])
            p = _proj(h, w_main, l, 0, P_COLS, 2048)
            st["v"] = _proj(h, w_main, l, W_VD, DIFF_V, DIFF_V, stacked=(depth, st["v"]))
            qd = _qk_rope(p, P_QD, qn[l], group_ones, st["tables"], BF16)
            st["k"] = _qk_rope(p, P_KD, kn[l], group_ones, st["tables"], F32, layer=l, stacked=(depth, st["k"]))
            ag, st["s"] = _gla(p, h, w_a, w_2, b_a, gla_norm, l, seq_len=slen, states_prev=st["s"],
                               depth=depth, s0=state_gla if si == 1 else None)
            if si == 0:
                ad = _attn_prompt(qd, st["k"], st["v"], li, diff_lambda, dn, l, seq_len=slen)
            else:
                ad = _attn_sample(qd, st["k"], st["v"], cache_diff_k, cache_diff_v, li, diff_lambda, dn, l,
                                  seq_len=slen)
            x = _mix_out(x, ag, ad, p, wgo, wdo, wo, l)
            st["x"] = _ffn(x, norm_ffn2[l][None], wg2, wu2, wd2, l)

    pr, sa = streams
    head_k = (DIFF_HEADS, 2 * DIFF_QK_HEAD)
    head_v = (DIFF_HEADS, DIFF_V_HEAD)
    return (pr["x"].reshape(nb, seq, d), sa["x"].reshape(ndb, dseq, d),
            pr["k"].reshape(depth, nb, seq, *head_k), pr["v"].reshape(depth, nb, seq, *head_v), pr["s"],
            sa["k"].reshape(depth, ndb, dseq, *head_k), sa["v"].reshape(depth, ndb, dseq, *head_v), sa["s"])
```

```python
import functools
import math

import jax
import jax.numpy as jnp
from jax import lax
from jax.experimental import pallas as pl
from jax.experimental.pallas import tpu as pltpu

F32 = jnp.float32
BF16 = jnp.bfloat16

EPS = 1e-6
CHUNK = 64
GLA_HEADS = 4
GLA_DK_HEAD = 256
GLA_DV_HEAD = 512
GLA_DK = GLA_HEADS * GLA_DK_HEAD
GLA_DV = GLA_HEADS * GLA_DV_HEAD
GLA_RANK = 16
GLA_TAU = 16.0
DIFF_HEADS = 8
DIFF_QK_HEAD = 64
DIFF_V_HEAD = 128
DIFF_QK = DIFF_HEADS * 2 * DIFF_QK_HEAD
DIFF_V = DIFF_HEADS * DIFF_V_HEAD
ROPE_DIM = 16
ROPE_THETA = 500000.0
FFN_RES = 0.5
ATTN_HEADS_PER_STEP = 4

LANES = 128
VMEM_BYTES = 64 * 1024 * 1024
VMEM_LIMIT_BYTES = VMEM_BYTES - 2 * 1024 * 1024

ROW_TILES = (1024, 512, 256, 128, 64)
MIX_ROW_TILES = (256, 128, 64)
ATTN_Q_TILES = (256, 128, 64)
FFN_TILE = 512
FFN_DOWN_CHUNK = 512
PROJ_TN = 2048
GLA_GROUP = (4, 2, 1)
GLA_GROUP_WITH_STATE_IN = (2, 1)

P_QG = 0
P_KG = P_QG + GLA_DK
P_VG = P_KG + GLA_DK
P_RG = P_VG + GLA_DV
P_QD = P_RG + GLA_DV
P_KD = P_QD + DIFF_QK
P_ZG = P_KD + DIFF_QK
P_ZD = P_ZG + GLA_DV
P_COLS = P_ZD + GLA_DV
W_VD = P_COLS


def _pick_tile(n, candidates):
    for c in candidates:
        if n % c == 0:
            return c
    raise ValueError(f"no tile in {candidates} divides {n}")


def _params(*semantics):
    return pltpu.CompilerParams(dimension_semantics=semantics, vmem_limit_bytes=VMEM_LIMIT_BYTES)


def _nt_dot(a, b):
    return lax.dot_general(a, b, (((1,), (1,)), ((), ())), preferred_element_type=F32)


def _tn_dot(a, b):
    return lax.dot_general(a, b, (((0,), (0,)), ((), ())), preferred_element_type=F32)


def _dot(a, b):
    return jnp.dot(a, b, preferred_element_type=F32)


def _rms(x):
    return x * lax.rsqrt(jnp.mean(x * x, axis=-1, keepdims=True) + EPS)


def _split2(x):
    hi = x.astype(BF16)
    lo = (x - hi.astype(F32)).astype(BF16)
    return hi, lo


def _stacked_out(prev, n_in):
    if prev is None:
        return [], [], {}
    return [prev], [pl.BlockSpec(memory_space=pl.ANY)], {n_in: 0}


def _ffn_body(*refs, tf, f_valid, dn, emit_h):
    if emit_h:
        x_ref, gain_ref, wg_ref, wu_ref, wd_ref, gain2_ref, o_ref, h2_ref, h_ref = refs
    else:
        x_ref, gain_ref, wg_ref, wu_ref, wd_ref, o_ref, h_ref = refs
    j = pl.program_id(1)
    last = pl.num_programs(1) - 1
    d = o_ref.shape[1]

    def step(width, where):
        h = h_ref[...]
        g = _dot(h, wg_ref[:, :width])
        u = _dot(h, wu_ref[:, :width])
        a = (g * jax.nn.sigmoid(g) * u).astype(BF16)
        for n0 in range(0, d, dn):
            cols = slice(n0, n0 + dn)
            y = _dot(a, wd_ref[:width, cols])
            if where == "first":
                o_ref[:, cols] = y
            elif where == "middle":
                o_ref[:, cols] += y
            else:
                o_ref[:, cols] = x_ref[:, cols] + FFN_RES * (o_ref[:, cols] + y)

    @pl.when(j == 0)
    def _():
        h_ref[...] = (_rms(x_ref[...]) * gain_ref[...]).astype(BF16)
        step(tf, "first")

    @pl.when(jnp.logical_and(j > 0, j < last))
    def _():
        step(tf, "middle")

    @pl.when(j == last)
    def _():
        step(f_valid, "last")
        if emit_h:
            h2_ref[...] = (_rms(o_ref[...]) * gain2_ref[...]).astype(BF16)


def _ffn(x, gain, wg, wu, wd, layer, gain_next=None):
    t, d = x.shape
    f = wg.shape[2]
    tm = _pick_tile(t, ROW_TILES)
    emit_h = gain_next is not None
    tf = FFN_TILE
    nf = pl.cdiv(f, tf)
    f_valid = f - (nf - 1) * tf
    assert nf >= 2 and f_valid % LANES == 0
    rows = pl.BlockSpec((tm, d), lambda i, j: (i, 0))
    vec = pl.BlockSpec((1, d), lambda i, j: (0, 0))
    in_specs = [
        rows, vec,
        pl.BlockSpec((None, d, tf), lambda i, j: (layer, 0, j)),
        pl.BlockSpec((None, d, tf), lambda i, j: (layer, 0, j)),
        pl.BlockSpec((None, tf, d), lambda i, j: (layer, j, 0)),
    ]
    inputs = [x, gain, wg, wu, wd]
    out_specs = [rows]
    out_shape = [jax.ShapeDtypeStruct((t, d), F32)]
    if emit_h:
        in_specs.append(vec)
        inputs.append(gain_next)
        out_specs.append(rows)
        out_shape.append(jax.ShapeDtypeStruct((t, d), BF16))
    out = pl.pallas_call(
        functools.partial(_ffn_body, tf=tf, f_valid=f_valid, dn=FFN_DOWN_CHUNK, emit_h=emit_h),
        grid=(t // tm, nf),
        in_specs=in_specs,
        out_specs=out_specs,
        out_shape=out_shape,
        scratch_shapes=[pltpu.VMEM((tm, d), BF16)],
        compiler_params=_params("parallel", "arbitrary"),
        name="ffn",
    )(*inputs)
    return out if emit_h else out[0]


def _proj_body(h_ref, w_ref, *rest):
    rest[-1][...] = _dot(h_ref[...], w_ref[...])


def _proj(h, w, layer, col0, n_cols, tn, stacked=None):
    t, d = h.shape
    tm = _pick_tile(t, ROW_TILES)
    assert col0 % tn == 0 and n_cols % tn == 0
    j0 = col0 // tn
    in_specs = [
        pl.BlockSpec((tm, d), lambda j, i: (i, 0)),
        pl.BlockSpec((None, d, tn), lambda j, i: (layer, 0, j0 + j)),
    ]
    if stacked is None:
        out_spec = pl.BlockSpec((tm, tn), lambda j, i: (i, j))
        out_shape = jax.ShapeDtypeStruct((t, n_cols), F32)
        extra_in, extra_specs, aliases = [], [], {}
    else:
        depth, prev = stacked
        out_spec = pl.BlockSpec((None, tm, tn), lambda j, i: (layer, i, j))
        out_shape = jax.ShapeDtypeStruct((depth, t, n_cols), F32)
        extra_in, extra_specs, aliases = _stacked_out(prev, 2)
    return pl.pallas_call(
        _proj_body,
        grid=(n_cols // tn, t // tm),
        in_specs=in_specs + extra_specs,
        out_specs=out_spec,
        out_shape=out_shape,
        input_output_aliases=aliases,
        compiler_params=_params("parallel", "parallel"),
        name="proj",
    )(h, w, *extra_in)


def _qk_rope_body(x_ref, gain_ref, ones_ref, cos_ref, sl_ref, sh_ref, *rest):
    o_ref = rest[-1]
    half = ROPE_DIM // 2
    cos, sin_lo, sin_hi = cos_ref[...], sl_ref[...], sh_ref[...]
    for c0 in range(0, x_ref.shape[1], LANES):
        x = x_ref[:, c0:c0 + LANES]
        s_hi, s_lo = _split2(x * x)
        ssum = _dot(jnp.concatenate([s_hi, s_lo], axis=1), ones_ref[...])
        y = x * lax.rsqrt(ssum * (1.0 / DIFF_QK_HEAD) + EPS) * gain_ref[:, c0:c0 + LANES]
        out = y * cos + pltpu.roll(y, LANES - half, 1) * sin_lo + pltpu.roll(y, half, 1) * sin_hi
        o_ref[:, c0:c0 + LANES] = out.astype(o_ref.dtype)


def _qk_rope(p, col, gain, group_ones, tables, out_dtype, layer=0, stacked=None):
    t = p.shape[0]
    tm = _pick_tile(t, ROW_TILES)
    cos, sin_lo, sin_hi = tables
    period = cos.shape[0] // tm
    const = lambda a: pl.BlockSpec(a.shape, lambda i: (0, 0))
    table = pl.BlockSpec((tm, LANES), lambda i: (i % period, 0))
    in_specs = [pl.BlockSpec((tm, DIFF_QK), lambda i: (i, col // DIFF_QK)), const(gain), const(group_ones),
                table, table, table]
    if stacked is None:
        out_spec = pl.BlockSpec((tm, DIFF_QK), lambda i: (i, 0))
        out_shape = jax.ShapeDtypeStruct((t, DIFF_QK), out_dtype)
        extra_in, extra_specs, aliases = [], [], {}
    else:
        depth, prev = stacked
        out_spec = pl.BlockSpec((None, tm, DIFF_QK), lambda i: (layer, i, 0))
        out_shape = jax.ShapeDtypeStruct((depth, t, DIFF_QK), out_dtype)
        extra_in, extra_specs, aliases = _stacked_out(prev, len(in_specs))
    return pl.pallas_call(
        _qk_rope_body,
        grid=(t // tm,),
        in_specs=in_specs + extra_specs,
        out_specs=out_spec,
        out_shape=out_shape,
        input_output_aliases=aliases,
        compiler_params=_params("parallel"),
        name="qk_rope",
    )(p, gain, group_ones, cos, sin_lo, sin_hi, *extra_in)


def _log_sigmoid(x):
    return jnp.minimum(x, 0.0) - jnp.log1p(jnp.exp(-jnp.abs(x)))


def _gla_body(*refs, n_group, has_s0):
    q_ref, k_ref, v_ref, r_ref, h_ref, wa_ref, w2_ref, ba_ref, gn_ref = refs[:9]
    s0_ref = refs[9] if has_s0 else None
    o_ref, sout_ref, st_ref = refs[-3:]
    c = pl.program_id(1)
    seqs = range(n_group)
    pairs = [(s, hd) for s in seqs for hd in range(GLA_HEADS)]
    ks = lambda hd: slice(hd * GLA_DK_HEAD, (hd + 1) * GLA_DK_HEAD)
    vs = lambda hd: slice(hd * GLA_DV_HEAD, (hd + 1) * GLA_DV_HEAD)

    @pl.when(c == 0)
    def _():
        if has_s0:
            for s, hd in pairs:
                st_ref[s, hd] = s0_ref[s, hd].T
        else:
            st_ref[...] = jnp.zeros_like(st_ref)

    row = lax.broadcasted_iota(jnp.int32, (CHUNK, CHUNK), 0)
    col = lax.broadcasted_iota(jnp.int32, (CHUNK, CHUNK), 1)
    causal = row >= col
    tri = jnp.where(causal, 1.0, 0.0).astype(BF16)

    a = [_dot(h_ref[s], wa_ref[...]).astype(BF16) for s in seqs]
    z = [_dot(a[s], w2_ref[...]) for s in seqs]
    g = [_log_sigmoid(z[s] + ba_ref[...]) * (1.0 / GLA_TAU) for s in seqs]
    g_hi = [g[s].astype(BF16) for s in seqs]
    g_r1 = [g[s] - g_hi[s].astype(F32) for s in seqs]
    g_mid = [g_r1[s].astype(BF16) for s in seqs]
    g_lo = [(g_r1[s] - g_mid[s].astype(F32)).astype(BF16) for s in seqs]
    parts = [[_dot(tri, part[s]) for s in seqs] for part in (g_hi, g_mid, g_lo)]
    b = [parts[0][s] + parts[1][s] + parts[2][s] for s in seqs]
    b_mid = [b[s][CHUNK // 2:CHUNK // 2 + 1] for s in seqs]
    b_end = [b[s][CHUNK - 1:CHUNK] for s in seqs]
    q = [q_ref[s] * (GLA_DK_HEAD ** -0.5) for s in seqs]
    q_in = [(q[s] * jnp.exp(b[s])).astype(BF16) for s in seqs]
    q_rel = [(q[s] * jnp.exp(b[s] - b_mid[s])).astype(BF16) for s in seqs]
    k_rel = [(k_ref[s] * jnp.exp(b_mid[s] - b[s])).astype(BF16) for s in seqs]
    k_end = [(k_ref[s] * jnp.exp(b_end[s] - b[s])).astype(BF16) for s in seqs]
    decay = [jnp.exp(b_end[s]) for s in seqs]

    vb = {(s, hd): v_ref[s, :, vs(hd)].astype(BF16) for s, hd in pairs}
    att = {(s, hd): _nt_dot(q_rel[s][:, ks(hd)], k_rel[s][:, ks(hd)]) for s, hd in pairs}
    st = {(s, hd): st_ref[s, hd] for s, hd in pairs}
    o = {(s, hd): _nt_dot(q_in[s][:, ks(hd)], st[s, hd].astype(BF16)) for s, hd in pairs}
    upd = {(s, hd): _tn_dot(vb[s, hd], k_end[s][:, ks(hd)]) for s, hd in pairs}
    att = {(s, hd): jnp.where(causal, att[s, hd], 0.0).astype(BF16) for s, hd in pairs}
    o = {(s, hd): o[s, hd] + _dot(att[s, hd], vb[s, hd]) for s, hd in pairs}
    for s, hd in pairs:
        st_ref[s, hd] = decay[s][:, ks(hd)] * st[s, hd] + upd[s, hd]
    for s, hd in pairs:
        r = r_ref[s, :, vs(hd)]
        y = _rms(o[s, hd]) * gn_ref[hd:hd + 1, :] * (r * jax.nn.sigmoid(r))
        o_ref[s, :, vs(hd)] = y.astype(BF16)

    @pl.when(c == pl.num_programs(1) - 1)
    def _():
        for s, hd in pairs:
            sout_ref[s, hd] = st_ref[s, hd].T


def _gla(p, h, wa, w2, ba, gn, layer, *, seq_len, states_prev, depth, s0=None):
    t, d = h.shape
    cps = seq_len // CHUNK
    n_seqs = t // seq_len
    n_group = _pick_tile(n_seqs, GLA_GROUP if s0 is None else GLA_GROUP_WITH_STATE_IN)
    p3 = p.reshape(n_seqs, seq_len, p.shape[1])
    h3 = h.reshape(n_seqs, seq_len, d)
    state_block = (None, n_group, GLA_HEADS, GLA_DK_HEAD, GLA_DV_HEAD)
    state_spec = pl.BlockSpec(state_block, lambda g, c: (layer, g, 0, 0, 0))
    cols = lambda width, col: pl.BlockSpec((n_group, CHUNK, width), lambda g, c: (g, c, col // width))
    in_specs = [
        cols(GLA_DK, P_QG), cols(GLA_DK, P_KG), cols(GLA_DV, P_VG), cols(GLA_DV, P_RG), cols(d, 0),
        pl.BlockSpec((None, d, LANES), lambda g, c: (layer, 0, 0)),
        pl.BlockSpec((None, LANES, GLA_DK), lambda g, c: (layer, 0, 0)),
        pl.BlockSpec((None, 1, GLA_DK), lambda g, c: (layer, 0, 0)),
        pl.BlockSpec((None, GLA_HEADS, GLA_DV_HEAD), lambda g, c: (layer, 0, 0)),
    ]
    inputs = [p3, p3, p3, p3, h3, wa, w2, ba, gn]
    if s0 is not None:
        in_specs.append(state_spec)
        inputs.append(s0)
    extra_in, extra_specs, aliases = _stacked_out(states_prev, len(inputs))
    aliases = {k: 1 for k in aliases}
    out, states = pl.pallas_call(
        functools.partial(_gla_body, n_group=n_group, has_s0=s0 is not None),
        grid=(n_seqs // n_group, cps),
        in_specs=in_specs + extra_specs,
        out_specs=[cols(GLA_DV, 0), state_spec],
        out_shape=[
            jax.ShapeDtypeStruct((n_seqs, seq_len, GLA_DV), BF16),
            jax.ShapeDtypeStruct((depth, n_seqs, GLA_HEADS, GLA_DK_HEAD, GLA_DV_HEAD), F32),
        ],
        input_output_aliases=aliases,
        scratch_shapes=[pltpu.VMEM((n_group, GLA_HEADS, GLA_DV_HEAD, GLA_DK_HEAD), F32)],
        compiler_params=_params("parallel", "arbitrary"),
        name="gla",
    )(*inputs, *extra_in)
    return out.reshape(t, GLA_DV), states


def _lambda_value(lam_ref, lambda_init):
    lp = lam_ref[...]
    d1 = jnp.sum(lp[0:1] * lp[1:2], axis=-1, keepdims=True)
    d2 = jnp.sum(lp[2:3] * lp[3:4], axis=-1, keepdims=True)
    return jnp.exp(d1) - jnp.exp(d2) + lambda_init


def _stack_maps(q):
    q = q.astype(F32) * (DIFF_QK_HEAD ** -0.5)
    lane = lax.broadcasted_iota(jnp.int32, q.shape, 1)
    first = lane < DIFF_QK_HEAD
    return jnp.concatenate([jnp.where(first, q, 0.0), jnp.where(first, 0.0, q)], axis=0).astype(BF16)


def _with_ones_column(v):
    lane = lax.broadcasted_iota(jnp.int32, v.shape, 1)
    return jnp.concatenate([v.astype(BF16), jnp.where(lane == 0, 1.0, 0.0).astype(BF16)], axis=1)


def _diff_attend(scores, values, lam):
    rows = scores[0].shape[0] // 2
    mx = functools.reduce(jnp.maximum, [jnp.max(s, axis=-1, keepdims=True) for s in scores])
    acc = functools.reduce(jnp.add, [_dot(jnp.exp(s - mx).astype(BF16), v) for s, v in zip(scores, values)])
    o = acc[:, :DIFF_V_HEAD] / acc[:, DIFF_V_HEAD:DIFF_V_HEAD + 1]
    return o[:rows] - lam * o[rows:]


def _attn_prompt_body(li_ref, q_ref, k_ref, v_ref, lam_ref, dn_ref, o_ref, *, tq, n_heads):
    seq_len = q_ref.shape[0]
    dv = DIFF_V_HEAD
    lambda_init = li_ref[0]
    lam = _lambda_value(lam_ref, lambda_init)
    row = lax.broadcasted_iota(jnp.int32, (2 * tq, tq), 0) % tq // CHUNK
    col = lax.broadcasted_iota(jnp.int32, (2 * tq, tq), 1) // CHUNK
    visible = row >= col
    n_tiles = seq_len // tq
    cols = lambda h: slice(h * dv, (h + 1) * dv)
    kb = [k_ref[:, cols(h)].astype(BF16) for h in range(n_heads)]
    vb = [_with_ones_column(v_ref[:, cols(h)]) for h in range(n_heads)]

    def tile_scores(h, i):
        lo, hi = i * tq, (i + 1) * tq
        qz = _stack_maps(q_ref[lo:hi, cols(h)])
        scores = [jnp.where(visible, _nt_dot(qz, kb[h][lo:hi]), -jnp.inf)]
        if i > 0:
            scores.append(_nt_dot(qz, kb[h][:lo]))
        return scores

    order = [i // 2 if i % 2 else n_tiles - 1 - i // 2 for i in range(n_tiles)]
    work = [(h, i) for h in range(n_heads) for i in order]
    scores = tile_scores(*work[0])
    for n, (h, i) in enumerate(work):
        lo, hi = i * tq, (i + 1) * tq
        ahead = tile_scores(*work[n + 1]) if n + 1 < len(work) else None
        values = [vb[h][lo:hi]] + ([vb[h][:lo]] if i > 0 else [])
        o = _diff_attend(scores, values, lam)
        o_ref[lo:hi, cols(h)] = (_rms(o) * dn_ref[h] * (1.0 - lambda_init)).astype(BF16)
        scores = ahead


def _attn_prompt(qd, k_all, v_all, li, lam, dn, layer, *, seq_len):
    t = qd.shape[0]
    n_seqs = t // seq_len
    tq = _pick_tile(seq_len, ATTN_Q_TILES)
    n_heads = ATTN_HEADS_PER_STEP
    width = n_heads * DIFF_V_HEAD
    kv = pl.BlockSpec((None, seq_len, width), lambda b, h: (layer, b, h))
    return pl.pallas_call(
        functools.partial(_attn_prompt_body, tq=tq, n_heads=n_heads),
        grid=(n_seqs, DIFF_HEADS // n_heads),
        in_specs=[
            pl.BlockSpec(memory_space=pltpu.SMEM),
            pl.BlockSpec((seq_len, width), lambda b, h: (b, h)),
            kv, kv,
            pl.BlockSpec((None, 4, DIFF_QK_HEAD), lambda b, h: (layer, 0, 0)),
            pl.BlockSpec((None, n_heads, 1, DIFF_V_HEAD), lambda b, h: (layer, h, 0, 0)),
        ],
        out_specs=pl.BlockSpec((seq_len, width), lambda b, h: (b, h)),
        out_shape=jax.ShapeDtypeStruct((t, DIFF_V), BF16),
        compiler_params=_params("parallel", "parallel"),
        name="attn_prompt",
    )(li, qd, k_all, v_all, lam, dn)


def _attn_sample_body(li_ref, q_ref, kc_ref, vc_ref, kn_ref, vn_ref, lam_ref, dn_ref, o_ref):
    past = kc_ref.shape[0] // DIFF_HEADS
    lambda_init = li_ref[0]
    lam = _lambda_value(lam_ref, lambda_init)
    cols = lambda h: slice(h * DIFF_V_HEAD, (h + 1) * DIFF_V_HEAD)
    head_rows = lambda h: pl.ds(h, past, stride=DIFF_HEADS)

    def head_scores(h):
        qz = _stack_maps(q_ref[:, cols(h)])
        return [_nt_dot(qz, kc_ref[head_rows(h), :].astype(BF16)), _nt_dot(qz, kn_ref[:, cols(h)].astype(BF16))]

    scores = head_scores(0)
    for h in range(DIFF_HEADS):
        ahead = head_scores(h + 1) if h + 1 < DIFF_HEADS else None
        values = [_with_ones_column(vc_ref[head_rows(h), :]), _with_ones_column(vn_ref[:, cols(h)])]
        o = _diff_attend(scores, values, lam)
        o_ref[:, cols(h)] = (_rms(o) * dn_ref[h] * (1.0 - lambda_init)).astype(BF16)
        scores = ahead


def _attn_sample(qd, k_all, v_all, cache_k, cache_v, li, lam, dn, layer, *, seq_len):
    t = qd.shape[0]
    n_seqs = t // seq_len
    depth, _, past, heads, dv = cache_k.shape
    rows = past * heads
    new = pl.BlockSpec((None, seq_len, heads * dv), lambda b: (layer, b, 0))
    old = pl.BlockSpec((None, None, rows, dv), lambda b: (layer, b, 0, 0))
    return pl.pallas_call(
        _attn_sample_body,
        grid=(n_seqs,),
        in_specs=[
            pl.BlockSpec(memory_space=pltpu.SMEM),
            pl.BlockSpec((seq_len, heads * dv), lambda b: (b, 0)),
            old, old, new, new,
            pl.BlockSpec((None, 4, DIFF_QK_HEAD), lambda b: (layer, 0, 0)),
            pl.BlockSpec((None, heads, 1, dv), lambda b: (layer, 0, 0, 0)),
        ],
        out_specs=pl.BlockSpec((seq_len, heads * dv), lambda b: (b, 0)),
        out_shape=jax.ShapeDtypeStruct((t, DIFF_V), BF16),
        compiler_params=_params("parallel"),
        name="attn_sample",
    )(li, qd, cache_k.reshape(depth, n_seqs, rows, dv), cache_v.reshape(depth, n_seqs, rows, dv),
      k_all, v_all, lam, dn)


def _mix_out_body(x_ref, ag_ref, ad_ref, zg_ref, zd_ref, wgo_ref, wdo_ref, wo_ref, o_ref):
    yg = _dot(ag_ref[...], wgo_ref[...])
    yd = _dot(ad_ref[...], wdo_ref[...])
    m = jax.nn.sigmoid(zg_ref[...]) * yg + jax.nn.sigmoid(zd_ref[...]) * yd
    o_ref[...] = x_ref[...] + _dot(m.astype(BF16), wo_ref[...])


def _mix_out(x, ag, ad, p, wgo, wdo, wo, layer):
    t, d = x.shape
    tm = _pick_tile(t, MIX_ROW_TILES)
    rows = lambda width, col=0: pl.BlockSpec((tm, width), lambda i: (i, col // width))
    resident = lambda w: pl.BlockSpec((None,) + w.shape[1:], lambda i: (layer, 0, 0),
                                      pipeline_mode=pl.Buffered(1))
    return pl.pallas_call(
        _mix_out_body,
        grid=(t // tm,),
        in_specs=[
            rows(d), rows(GLA_DV), rows(DIFF_V), rows(d, P_ZG), rows(d, P_ZD),
            resident(wgo), resident(wdo), resident(wo),
        ],
        out_specs=rows(d),
        out_shape=jax.ShapeDtypeStruct((t, d), F32),
        compiler_params=_params("parallel"),
        name="mix_out",
    )(x, ag, ad, p, p, wgo, wdo, wo)


def _rope_tables(pos):
    half = ROPE_DIM // 2
    inv = ROPE_THETA ** (-jnp.arange(half, dtype=F32) * 2.0 / ROPE_DIM)
    ang = pos.astype(F32)[:, None] * inv[None, :]
    cos, sin = jnp.cos(ang), jnp.sin(ang)
    n = pos.shape[0]
    rest = DIFF_QK_HEAD - ROPE_DIM
    zeros = jnp.zeros((n, half), F32)
    pad = lambda parts, fill: jnp.tile(
        jnp.concatenate(parts + [jnp.full((n, rest), fill, F32)], axis=1), (1, LANES // DIFF_QK_HEAD))
    return pad([cos, cos], 1.0), pad([-sin, zeros], 0.0), pad([zeros, sin], 0.0)


def kernel(x_prompt, x_sample, cache_diff_k, cache_diff_v, state_gla, norm_ffn1, w_ffn1_gate, w_ffn1_up,
           w_ffn1_down, norm_mix, w_in, w_alpha2, b_alpha, q_norm, k_norm, diff_lambda, gla_norm, diff_norm,
           w_gla_o, w_diff_o, w_out, norm_ffn2, w_ffn2_gate, w_ffn2_up, w_ffn2_down):
    nb, seq, d = x_prompt.shape
    ndb, dseq, _ = x_sample.shape
    depth = w_in.shape[0]
    past = cache_diff_k.shape[2]
    assert seq % CHUNK == 0 and dseq == CHUNK and d == GLA_DV

    bf = lambda w: w.astype(BF16)
    wg1, wu1, wd1 = bf(w_ffn1_gate), bf(w_ffn1_up), bf(w_ffn1_down)
    wg2, wu2, wd2 = bf(w_ffn2_gate), bf(w_ffn2_up), bf(w_ffn2_down)
    a0 = P_QD
    qd0 = a0 + GLA_RANK
    vd0 = qd0 + 2 * DIFF_QK
    zg0 = vd0 + DIFF_V
    w_main = jnp.concatenate([w_in[:, :, :a0], w_in[:, :, qd0:vd0], w_in[:, :, zg0:], w_in[:, :, vd0:zg0]],
                             axis=-1).astype(BF16)
    w_a = jnp.pad(w_in[:, :, a0:a0 + GLA_RANK], ((0, 0), (0, 0), (0, LANES - GLA_RANK))).astype(BF16)
    w_2 = jnp.pad(w_alpha2, ((0, 0), (0, LANES - GLA_RANK), (0, 0))).astype(BF16)
    b_a = b_alpha.reshape(depth, 1, GLA_DK)
    wgo, wdo, wo = bf(w_gla_o), bf(w_diff_o), bf(w_out)
    n_groups = DIFF_QK // DIFF_QK_HEAD
    lane_group = (jnp.arange(2 * LANES) % LANES) // DIFF_QK_HEAD
    group_ones = (lane_group[:, None] == lane_group[None, :LANES]).astype(BF16)
    qn = jnp.tile(q_norm, (1, n_groups)).reshape(depth, 1, DIFF_QK)
    kn = jnp.tile(k_norm, (1, n_groups)).reshape(depth, 1, DIFF_QK)
    dn = diff_norm.reshape(depth, DIFF_HEADS, 1, DIFF_V_HEAD)

    streams = []
    for x0, n_seqs, slen, start in ((x_prompt, nb, seq, 0), (x_sample, ndb, dseq, past)):
        rows = n_seqs * slen
        tm = _pick_tile(rows, ROW_TILES)
        period = max(slen, tm)
        assert period % slen == 0 and period % tm == 0
        tables = _rope_tables(start + jnp.arange(period) % slen)
        streams.append(dict(x=x0.reshape(rows, d), slen=slen, tables=tables, k=None, v=None, s=None))

    for l in range(depth):
        lambda_init = 0.8 - 0.6 * math.exp(-0.3 * l)
        li = jnp.full((1,), lambda_init, F32)
        for si, st in enumerate(streams):
            slen = st["slen"]
            x, h = _ffn(st["x"], norm_ffn1[l][None], wg1, wu1, wd1, l, gain_next=norm_mix[l][None])
            p = _proj(h, w_main, l, 0, P_COLS, PROJ_TN)
            st["v"] = _proj(h, w_main, l, W_VD, DIFF_V, DIFF_V, stacked=(depth, st["v"]))
            qd = _qk_rope(p, P_QD, qn[l], group_ones, st["tables"], BF16)
            st["k"] = _qk_rope(p, P_KD, kn[l], group_ones, st["tables"], F32, layer=l, stacked=(depth, st["k"]))
            ag, st["s"] = _gla(p, h, w_a, w_2, b_a, gla_norm, l, seq_len=slen, states_prev=st["s"],
                               depth=depth, s0=state_gla if si == 1 else None)
            if si == 0:
                ad = _attn_prompt(qd, st["k"], st["v"], li, diff_lambda, dn, l, seq_len=slen)
            else:
                ad = _attn_sample(qd, st["k"], st["v"], cache_diff_k, cache_diff_v, li, diff_lambda, dn, l,
                                  seq_len=slen)
            x = _mix_out(x, ag, ad, p, wgo, wdo, wo, l)
            st["x"] = _ffn(x, norm_ffn2[l][None], wg2, wu2, wd2, l)

    pr, sa = streams
    head_k = (DIFF_HEADS, 2 * DIFF_QK_HEAD)
    head_v = (DIFF_HEADS, DIFF_V_HEAD)
    return (pr["x"].reshape(nb, seq, d), sa["x"].reshape(ndb, dseq, d),
            pr["k"].reshape(depth, nb, seq, *head_k), pr["v"].reshape(depth, nb, seq, *head_v), pr["s"],
            sa["k"].reshape(depth, ndb, dseq, *head_k), sa["v"].reshape(depth, ndb, dseq, *head_v), sa["s"])
```

```python
import functools
import math

import jax
import jax.numpy as jnp
from jax import lax
from jax.experimental import pallas as pl
from jax.experimental.pallas import tpu as pltpu

F32 = jnp.float32
BF16 = jnp.bfloat16

EPS = 1e-6
CHUNK = 64
GLA_HEADS = 4
GLA_DK_HEAD = 256
GLA_DV_HEAD = 512
GLA_DK = GLA_HEADS * GLA_DK_HEAD
GLA_DV = GLA_HEADS * GLA_DV_HEAD
GLA_RANK = 16
GLA_TAU = 16.0
DIFF_HEADS = 8
DIFF_QK_HEAD = 64
DIFF_V_HEAD = 128
DIFF_QK = DIFF_HEADS * 2 * DIFF_QK_HEAD
DIFF_V = DIFF_HEADS * DIFF_V_HEAD
ROPE_DIM = 16
ROPE_THETA = 500000.0
FFN_RES = 0.5
ATTN_HEADS_PER_STEP = 4

LANES = 128
VMEM_BYTES = 64 * 1024 * 1024
VMEM_LIMIT_BYTES = VMEM_BYTES - 2 * 1024 * 1024

ROW_TILES = (1024, 512, 256, 128, 64)
MIX_ROW_TILES = (256, 128, 64)
ATTN_Q_TILES = (256, 128, 64)
FFN_TILE = 512
FFN_DOWN_CHUNK = 512
PROJ_TN = 2048
GLA_GROUP = (4, 2, 1)
GLA_GROUP_WITH_STATE_IN = (2, 1)

P_QG = 0
P_KG = P_QG + GLA_DK
P_VG = P_KG + GLA_DK
P_RG = P_VG + GLA_DV
P_QD = P_RG + GLA_DV
P_KD = P_QD + DIFF_QK
P_ZG = P_KD + DIFF_QK
P_ZD = P_ZG + GLA_DV
P_COLS = P_ZD + GLA_DV
W_VD = P_COLS


def _pick_tile(n, candidates):
    for c in candidates:
        if n % c == 0:
            return c
    raise ValueError(f"no tile in {candidates} divides {n}")


def _params(*semantics):
    return pltpu.CompilerParams(dimension_semantics=semantics, vmem_limit_bytes=VMEM_LIMIT_BYTES)


def _nt_dot(a, b):
    return lax.dot_general(a, b, (((1,), (1,)), ((), ())), preferred_element_type=F32)


def _tn_dot(a, b):
    return lax.dot_general(a, b, (((0,), (0,)), ((), ())), preferred_element_type=F32)


def _dot(a, b):
    return jnp.dot(a, b, preferred_element_type=F32)


def _rms(x):
    return x * lax.rsqrt(jnp.mean(x * x, axis=-1, keepdims=True) + EPS)


def _split2(x):
    hi = x.astype(BF16)
    lo = (x - hi.astype(F32)).astype(BF16)
    return hi, lo


def _stacked_out(prev, n_in):
    if prev is None:
        return [], [], {}
    return [prev], [pl.BlockSpec(memory_space=pl.ANY)], {n_in: 0}


def _ffn_body(*refs, tf, f_valid, dn, emit_h):
    if emit_h:
        x_ref, gain_ref, wg_ref, wu_ref, wd_ref, gain2_ref, o_ref, h2_ref, h_ref = refs
    else:
        x_ref, gain_ref, wg_ref, wu_ref, wd_ref, o_ref, h_ref = refs
    j = pl.program_id(1)
    last = pl.num_programs(1) - 1
    d = o_ref.shape[1]

    def step(width, where):
        h = h_ref[...]
        g = _dot(h, wg_ref[:, :width])
        u = _dot(h, wu_ref[:, :width])
        a = (g * jax.nn.sigmoid(g) * u).astype(BF16)
        for n0 in range(0, d, dn):
            cols = slice(n0, n0 + dn)
            y = _dot(a, wd_ref[:width, cols])
            if where == "first":
                o_ref[:, cols] = y
            elif where == "middle":
                o_ref[:, cols] += y
            else:
                o_ref[:, cols] = x_ref[:, cols] + FFN_RES * (o_ref[:, cols] + y)

    @pl.when(j == 0)
    def _():
        h_ref[...] = (_rms(x_ref[...]) * gain_ref[...]).astype(BF16)
        step(tf, "first")

    @pl.when(jnp.logical_and(j > 0, j < last))
    def _():
        step(tf, "middle")

    @pl.when(j == last)
    def _():
        step(f_valid, "last")
        if emit_h:
            h2_ref[...] = (_rms(o_ref[...]) * gain2_ref[...]).astype(BF16)


def _ffn(x, gain, wg, wu, wd, layer, gain_next=None):
    t, d = x.shape
    f = wg.shape[2]
    tm = _pick_tile(t, ROW_TILES)
    emit_h = gain_next is not None
    tf = FFN_TILE
    nf = pl.cdiv(f, tf)
    f_valid = f - (nf - 1) * tf
    assert nf >= 2 and f_valid % LANES == 0
    rows = pl.BlockSpec((tm, d), lambda i, j: (i, 0))
    vec = pl.BlockSpec((1, d), lambda i, j: (0, 0))
    in_specs = [
        rows, vec,
        pl.BlockSpec((None, d, tf), lambda i, j: (layer, 0, j)),
        pl.BlockSpec((None, d, tf), lambda i, j: (layer, 0, j)),
        pl.BlockSpec((None, tf, d), lambda i, j: (layer, j, 0)),
    ]
    inputs = [x, gain, wg, wu, wd]
    out_specs = [rows]
    out_shape = [jax.ShapeDtypeStruct((t, d), F32)]
    if emit_h:
        in_specs.append(vec)
        inputs.append(gain_next)
        out_specs.append(rows)
        out_shape.append(jax.ShapeDtypeStruct((t, d), BF16))
    out = pl.pallas_call(
        functools.partial(_ffn_body, tf=tf, f_valid=f_valid, dn=FFN_DOWN_CHUNK, emit_h=emit_h),
        grid=(t // tm, nf),
        in_specs=in_specs,
        out_specs=out_specs,
        out_shape=out_shape,
        scratch_shapes=[pltpu.VMEM((tm, d), BF16)],
        compiler_params=_params("parallel", "arbitrary"),
        name="ffn",
    )(*inputs)
    return out if emit_h else out[0]


def _proj_body(h_ref, w_ref, *rest):
    rest[-1][...] = _dot(h_ref[...], w_ref[...])


def _proj(h, w, layer, col0, n_cols, tn, stacked=None):
    t, d = h.shape
    tm = _pick_tile(t, ROW_TILES)
    assert col0 % tn == 0 and n_cols % tn == 0
    j0 = col0 // tn
    in_specs = [
        pl.BlockSpec((tm, d), lambda j, i: (i, 0)),
        pl.BlockSpec((None, d, tn), lambda j, i: (layer, 0, j0 + j)),
    ]
    if stacked is None:
        out_spec = pl.BlockSpec((tm, tn), lambda j, i: (i, j))
        out_shape = jax.ShapeDtypeStruct((t, n_cols), F32)
        extra_in, extra_specs, aliases = [], [], {}
    else:
        depth, prev = stacked
        out_spec = pl.BlockSpec((None, tm, tn), lambda j, i: (layer, i, j))
        out_shape = jax.ShapeDtypeStruct((depth, t, n_cols), F32)
        extra_in, extra_specs, aliases = _stacked_out(prev, 2)
    return pl.pallas_call(
        _proj_body,
        grid=(n_cols // tn, t // tm),
        in_specs=in_specs + extra_specs,
        out_specs=out_spec,
        out_shape=out_shape,
        input_output_aliases=aliases,
        compiler_params=_params("parallel", "parallel"),
        name="proj",
    )(h, w, *extra_in)


def _qk_rope_body(x_ref, gain_ref, ones_ref, cos_ref, sl_ref, sh_ref, *rest):
    o_ref = rest[-1]
    half = ROPE_DIM // 2
    cos, sin_lo, sin_hi = cos_ref[...], sl_ref[...], sh_ref[...]
    for c0 in range(0, x_ref.shape[1], LANES):
        x = x_ref[:, c0:c0 + LANES]
        s_hi, s_lo = _split2(x * x)
        ssum = _dot(jnp.concatenate([s_hi, s_lo], axis=1), ones_ref[...])
        y = x * lax.rsqrt(ssum * (1.0 / DIFF_QK_HEAD) + EPS) * gain_ref[:, c0:c0 + LANES]
        out = y * cos + pltpu.roll(y, LANES - half, 1) * sin_lo + pltpu.roll(y, half, 1) * sin_hi
        o_ref[:, c0:c0 + LANES] = out.astype(o_ref.dtype)


def _qk_rope(p, col, gain, group_ones, tables, out_dtype, layer=0, stacked=None):
    t = p.shape[0]
    tm = _pick_tile(t, ROW_TILES)
    cos, sin_lo, sin_hi = tables
    period = cos.shape[0] // tm
    const = lambda a: pl.BlockSpec(a.shape, lambda i: (0, 0))
    table = pl.BlockSpec((tm, LANES), lambda i: (i % period, 0))
    in_specs = [pl.BlockSpec((tm, DIFF_QK), lambda i: (i, col // DIFF_QK)), const(gain), const(group_ones),
                table, table, table]
    if stacked is None:
        out_spec = pl.BlockSpec((tm, DIFF_QK), lambda i: (i, 0))
        out_shape = jax.ShapeDtypeStruct((t, DIFF_QK), out_dtype)
        extra_in, extra_specs, aliases = [], [], {}
    else:
        depth, prev = stacked
        out_spec = pl.BlockSpec((None, tm, DIFF_QK), lambda i: (layer, i, 0))
        out_shape = jax.ShapeDtypeStruct((depth, t, DIFF_QK), out_dtype)
        extra_in, extra_specs, aliases = _stacked_out(prev, len(in_specs))
    return pl.pallas_call(
        _qk_rope_body,
        grid=(t // tm,),
        in_specs=in_specs + extra_specs,
        out_specs=out_spec,
        out_shape=out_shape,
        input_output_aliases=aliases,
        compiler_params=_params("parallel"),
        name="qk_rope",
    )(p, gain, group_ones, cos, sin_lo, sin_hi, *extra_in)


def _log_sigmoid(x):
    return jnp.minimum(x, 0.0) - jnp.log1p(jnp.exp(-jnp.abs(x)))


def _gla_body(*refs, n_group, has_s0):
    q_ref, k_ref, v_ref, r_ref, h_ref, wa_ref, w2_ref, ba_ref, gn_ref = refs[:9]
    s0_ref = refs[9] if has_s0 else None
    o_ref, sout_ref, st_ref = refs[-3:]
    c = pl.program_id(1)
    seqs = range(n_group)
    pairs = [(s, hd) for s in seqs for hd in range(GLA_HEADS)]
    ks = lambda hd: slice(hd * GLA_DK_HEAD, (hd + 1) * GLA_DK_HEAD)
    vs = lambda hd: slice(hd * GLA_DV_HEAD, (hd + 1) * GLA_DV_HEAD)

    @pl.when(c == 0)
    def _():
        if has_s0:
            for s, hd in pairs:
                st_ref[s, hd] = s0_ref[s, hd].T
        else:
            st_ref[...] = jnp.zeros_like(st_ref)

    row = lax.broadcasted_iota(jnp.int32, (CHUNK, CHUNK), 0)
    col = lax.broadcasted_iota(jnp.int32, (CHUNK, CHUNK), 1)
    causal = row >= col
    tri = jnp.where(causal, 1.0, 0.0).astype(BF16)

    a = [_dot(h_ref[s], wa_ref[...]).astype(BF16) for s in seqs]
    z = [_dot(a[s], w2_ref[...]) for s in seqs]
    g = [_log_sigmoid(z[s] + ba_ref[...]) * (1.0 / GLA_TAU) for s in seqs]
    g_hi = [g[s].astype(BF16) for s in seqs]
    g_r1 = [g[s] - g_hi[s].astype(F32) for s in seqs]
    g_mid = [g_r1[s].astype(BF16) for s in seqs]
    g_lo = [(g_r1[s] - g_mid[s].astype(F32)).astype(BF16) for s in seqs]
    parts = [[_dot(tri, part[s]) for s in seqs] for part in (g_hi, g_mid, g_lo)]
    b = [parts[0][s] + parts[1][s] + parts[2][s] for s in seqs]
    b_mid = [b[s][CHUNK // 2:CHUNK // 2 + 1] for s in seqs]
    b_end = [b[s][CHUNK - 1:CHUNK] for s in seqs]
    q = [q_ref[s] * (GLA_DK_HEAD ** -0.5) for s in seqs]
    q_in = [(q[s] * jnp.exp(b[s])).astype(BF16) for s in seqs]
    q_rel = [(q[s] * jnp.exp(b[s] - b_mid[s])).astype(BF16) for s in seqs]
    k_rel = [(k_ref[s] * jnp.exp(b_mid[s] - b[s])).astype(BF16) for s in seqs]
    k_end = [(k_ref[s] * jnp.exp(b_end[s] - b[s])).astype(BF16) for s in seqs]
    decay = [jnp.exp(b_end[s]) for s in seqs]

    vb = lambda s, hd: v_ref[s, :, vs(hd)].astype(BF16)
    att = {(s, hd): _nt_dot(q_rel[s][:, ks(hd)], k_rel[s][:, ks(hd)]) for s, hd in pairs}
    att = {(s, hd): jnp.where(causal, att[s, hd], 0.0).astype(BF16) for s, hd in pairs}
    for s, hd in pairs:
        o = _nt_dot(q_in[s][:, ks(hd)], st_ref[s, hd].astype(BF16)) + _dot(att[s, hd], vb(s, hd))
        r = r_ref[s, :, vs(hd)]
        y = _rms(o) * gn_ref[hd:hd + 1, :] * (r * jax.nn.sigmoid(r))
        o_ref[s, :, vs(hd)] = y.astype(BF16)
    for s, hd in pairs:
        st_ref[s, hd] = decay[s][:, ks(hd)] * st_ref[s, hd] + _tn_dot(vb(s, hd), k_end[s][:, ks(hd)])

    @pl.when(c == pl.num_programs(1) - 1)
    def _():
        for s, hd in pairs:
            sout_ref[s, hd] = st_ref[s, hd].T


def _gla(p, h, wa, w2, ba, gn, layer, *, seq_len, states_prev, depth, s0=None):
    t, d = h.shape
    cps = seq_len // CHUNK
    n_seqs = t // seq_len
    n_group = _pick_tile(n_seqs, GLA_GROUP if s0 is None else GLA_GROUP_WITH_STATE_IN)
    p3 = p.reshape(n_seqs, seq_len, p.shape[1])
    h3 = h.reshape(n_seqs, seq_len, d)
    state_block = (None, n_group, GLA_HEADS, GLA_DK_HEAD, GLA_DV_HEAD)
    state_spec = pl.BlockSpec(state_block, lambda g, c: (layer, g, 0, 0, 0))
    cols = lambda width, col: pl.BlockSpec((n_group, CHUNK, width), lambda g, c: (g, c, col // width))
    in_specs = [
        cols(GLA_DK, P_QG), cols(GLA_DK, P_KG), cols(GLA_DV, P_VG), cols(GLA_DV, P_RG), cols(d, 0),
        pl.BlockSpec((None, d, LANES), lambda g, c: (layer, 0, 0)),
        pl.BlockSpec((None, LANES, GLA_DK), lambda g, c: (layer, 0, 0)),
        pl.BlockSpec((None, 1, GLA_DK), lambda g, c: (layer, 0, 0)),
        pl.BlockSpec((None, GLA_HEADS, GLA_DV_HEAD), lambda g, c: (layer, 0, 0)),
    ]
    inputs = [p3, p3, p3, p3, h3, wa, w2, ba, gn]
    if s0 is not None:
        in_specs.append(state_spec)
        inputs.append(s0)
    extra_in, extra_specs, aliases = _stacked_out(states_prev, len(inputs))
    aliases = {k: 1 for k in aliases}
    out, states = pl.pallas_call(
        functools.partial(_gla_body, n_group=n_group, has_s0=s0 is not None),
        grid=(n_seqs // n_group, cps),
        in_specs=in_specs + extra_specs,
        out_specs=[cols(GLA_DV, 0), state_spec],
        out_shape=[
            jax.ShapeDtypeStruct((n_seqs, seq_len, GLA_DV), BF16),
            jax.ShapeDtypeStruct((depth, n_seqs, GLA_HEADS, GLA_DK_HEAD, GLA_DV_HEAD), F32),
        ],
        input_output_aliases=aliases,
        scratch_shapes=[pltpu.VMEM((n_group, GLA_HEADS, GLA_DV_HEAD, GLA_DK_HEAD), F32)],
        compiler_params=_params("parallel", "arbitrary"),
        name="gla",
    )(*inputs, *extra_in)
    return out.reshape(t, GLA_DV), states


def _lambda_value(lam_ref, lambda_init):
    lp = lam_ref[...]
    d1 = jnp.sum(lp[0:1] * lp[1:2], axis=-1, keepdims=True)
    d2 = jnp.sum(lp[2:3] * lp[3:4], axis=-1, keepdims=True)
    return jnp.exp(d1) - jnp.exp(d2) + lambda_init


def _stack_maps(q):
    q = q.astype(F32) * (DIFF_QK_HEAD ** -0.5)
    lane = lax.broadcasted_iota(jnp.int32, q.shape, 1)
    first = lane < DIFF_QK_HEAD
    return jnp.concatenate([jnp.where(first, q, 0.0), jnp.where(first, 0.0, q)], axis=0).astype(BF16)


def _with_ones_column(v):
    lane = lax.broadcasted_iota(jnp.int32, v.shape, 1)
    return jnp.concatenate([v.astype(BF16), jnp.where(lane == 0, 1.0, 0.0).astype(BF16)], axis=1)


def _diff_attend(scores, values, lam):
    rows = scores[0].shape[0] // 2
    mx = functools.reduce(jnp.maximum, [jnp.max(s, axis=-1, keepdims=True) for s in scores])
    acc = functools.reduce(jnp.add, [_dot(jnp.exp(s - mx).astype(BF16), v) for s, v in zip(scores, values)])
    o = acc[:, :DIFF_V_HEAD] / acc[:, DIFF_V_HEAD:DIFF_V_HEAD + 1]
    return o[:rows] - lam * o[rows:]


def _attn_prompt_body(li_ref, q_ref, k_ref, v_ref, lam_ref, dn_ref, o_ref, *, tq, n_heads):
    seq_len = q_ref.shape[0]
    dv = DIFF_V_HEAD
    lambda_init = li_ref[0]
    lam = _lambda_value(lam_ref, lambda_init)
    row = lax.broadcasted_iota(jnp.int32, (2 * tq, tq), 0) % tq // CHUNK
    col = lax.broadcasted_iota(jnp.int32, (2 * tq, tq), 1) // CHUNK
    visible = row >= col
    n_tiles = seq_len // tq
    cols = lambda h: slice(h * dv, (h + 1) * dv)
    kb = [k_ref[:, cols(h)].astype(BF16) for h in range(n_heads)]
    vb = [_with_ones_column(v_ref[:, cols(h)]) for h in range(n_heads)]

    def tile_scores(h, i):
        lo, hi = i * tq, (i + 1) * tq
        qz = _stack_maps(q_ref[lo:hi, cols(h)])
        scores = [jnp.where(visible, _nt_dot(qz, kb[h][lo:hi]), -jnp.inf)]
        if i > 0:
            scores.append(_nt_dot(qz, kb[h][:lo]))
        return scores

    order = [i // 2 if i % 2 else n_tiles - 1 - i // 2 for i in range(n_tiles)]
    work = [(h, i) for h in range(n_heads) for i in order]
    scores = tile_scores(*work[0])
    for n, (h, i) in enumerate(work):
        lo, hi = i * tq, (i + 1) * tq
        ahead = tile_scores(*work[n + 1]) if n + 1 < len(work) else None
        values = [vb[h][lo:hi]] + ([vb[h][:lo]] if i > 0 else [])
        o = _diff_attend(scores, values, lam)
        o_ref[lo:hi, cols(h)] = (_rms(o) * dn_ref[h] * (1.0 - lambda_init)).astype(BF16)
        scores = ahead


def _attn_prompt(qd, k_all, v_all, li, lam, dn, layer, *, seq_len):
    t = qd.shape[0]
    n_seqs = t // seq_len
    tq = _pick_tile(seq_len, ATTN_Q_TILES)
    n_heads = ATTN_HEADS_PER_STEP
    width = n_heads * DIFF_V_HEAD
    kv = pl.BlockSpec((None, seq_len, width), lambda b, h: (layer, b, h))
    return pl.pallas_call(
        functools.partial(_attn_prompt_body, tq=tq, n_heads=n_heads),
        grid=(n_seqs, DIFF_HEADS // n_heads),
        in_specs=[
            pl.BlockSpec(memory_space=pltpu.SMEM),
            pl.BlockSpec((seq_len, width), lambda b, h: (b, h)),
            kv, kv,
            pl.BlockSpec((None, 4, DIFF_QK_HEAD), lambda b, h: (layer, 0, 0)),
            pl.BlockSpec((None, n_heads, 1, DIFF_V_HEAD), lambda b, h: (layer, h, 0, 0)),
        ],
        out_specs=pl.BlockSpec((seq_len, width), lambda b, h: (b, h)),
        out_shape=jax.ShapeDtypeStruct((t, DIFF_V), BF16),
        compiler_params=_params("parallel", "parallel"),
        name="attn_prompt",
    )(li, qd, k_all, v_all, lam, dn)


def _attn_sample_body(li_ref, q_ref, kc_ref, vc_ref, kn_ref, vn_ref, lam_ref, dn_ref, o_ref):
    past = kc_ref.shape[0] // DIFF_HEADS
    lambda_init = li_ref[0]
    lam = _lambda_value(lam_ref, lambda_init)
    cols = lambda h: slice(h * DIFF_V_HEAD, (h + 1) * DIFF_V_HEAD)
    head_rows = lambda h: pl.ds(h, past, stride=DIFF_HEADS)

    def head_scores(h):
        qz = _stack_maps(q_ref[:, cols(h)])
        return [_nt_dot(qz, kc_ref[head_rows(h), :].astype(BF16)), _nt_dot(qz, kn_ref[:, cols(h)].astype(BF16))]

    scores = head_scores(0)
    for h in range(DIFF_HEADS):
        ahead = head_scores(h + 1) if h + 1 < DIFF_HEADS else None
        values = [_with_ones_column(vc_ref[head_rows(h), :]), _with_ones_column(vn_ref[:, cols(h)])]
        o = _diff_attend(scores, values, lam)
        o_ref[:, cols(h)] = (_rms(o) * dn_ref[h] * (1.0 - lambda_init)).astype(BF16)
        scores = ahead


def _attn_sample(qd, k_all, v_all, cache_k, cache_v, li, lam, dn, layer, *, seq_len):
    t = qd.shape[0]
    n_seqs = t // seq_len
    depth, _, past, heads, dv = cache_k.shape
    rows = past * heads
    new = pl.BlockSpec((None, seq_len, heads * dv), lambda b: (layer, b, 0))
    old = pl.BlockSpec((None, None, rows, dv), lambda b: (layer, b, 0, 0))
    return pl.pallas_call(
        _attn_sample_body,
        grid=(n_seqs,),
        in_specs=[
            pl.BlockSpec(memory_space=pltpu.SMEM),
            pl.BlockSpec((seq_len, heads * dv), lambda b: (b, 0)),
            old, old, new, new,
            pl.BlockSpec((None, 4, DIFF_QK_HEAD), lambda b: (layer, 0, 0)),
            pl.BlockSpec((None, heads, 1, dv), lambda b: (layer, 0, 0, 0)),
        ],
        out_specs=pl.BlockSpec((seq_len, heads * dv), lambda b: (b, 0)),
        out_shape=jax.ShapeDtypeStruct((t, DIFF_V), BF16),
        compiler_params=_params("parallel"),
        name="attn_sample",
    )(li, qd, cache_k.reshape(depth, n_seqs, rows, dv), cache_v.reshape(depth, n_seqs, rows, dv),
      k_all, v_all, lam, dn)


def _mix_out_body(x_ref, ag_ref, ad_ref, zg_ref, zd_ref, wgo_ref, wdo_ref, wo_ref, o_ref):
    yg = _dot(ag_ref[...], wgo_ref[...])
    yd = _dot(ad_ref[...], wdo_ref[...])
    m = jax.nn.sigmoid(zg_ref[...]) * yg + jax.nn.sigmoid(zd_ref[...]) * yd
    o_ref[...] = x_ref[...] + _dot(m.astype(BF16), wo_ref[...])


def _mix_out(x, ag, ad, p, wgo, wdo, wo, layer):
    t, d = x.shape
    tm = _pick_tile(t, MIX_ROW_TILES)
    rows = lambda width, col=0: pl.BlockSpec((tm, width), lambda i: (i, col // width))
    resident = lambda w: pl.BlockSpec((None,) + w.shape[1:], lambda i: (layer, 0, 0),
                                      pipeline_mode=pl.Buffered(1))
    return pl.pallas_call(
        _mix_out_body,
        grid=(t // tm,),
        in_specs=[
            rows(d), rows(GLA_DV), rows(DIFF_V), rows(d, P_ZG), rows(d, P_ZD),
            resident(wgo), resident(wdo), resident(wo),
        ],
        out_specs=rows(d),
        out_shape=jax.ShapeDtypeStruct((t, d), F32),
        compiler_params=_params("parallel"),
        name="mix_out",
    )(x, ag, ad, p, p, wgo, wdo, wo)


def _rope_tables(pos):
    half = ROPE_DIM // 2
    inv = ROPE_THETA ** (-jnp.arange(half, dtype=F32) * 2.0 / ROPE_DIM)
    ang = pos.astype(F32)[:, None] * inv[None, :]
    cos, sin = jnp.cos(ang), jnp.sin(ang)
    n = pos.shape[0]
    rest = DIFF_QK_HEAD - ROPE_DIM
    zeros = jnp.zeros((n, half), F32)
    pad = lambda parts, fill: jnp.tile(
        jnp.concatenate(parts + [jnp.full((n, rest), fill, F32)], axis=1), (1, LANES // DIFF_QK_HEAD))
    return pad([cos, cos], 1.0), pad([-sin, zeros], 0.0), pad([zeros, sin], 0.0)


def kernel(x_prompt, x_sample, cache_diff_k, cache_diff_v, state_gla, norm_ffn1, w_ffn1_gate, w_ffn1_up,
           w_ffn1_down, norm_mix, w_in, w_alpha2, b_alpha, q_norm, k_norm, diff_lambda, gla_norm, diff_norm,
           w_gla_o, w_diff_o, w_out, norm_ffn2, w_ffn2_gate, w_ffn2_up, w_ffn2_down):
    nb, seq, d = x_prompt.shape
    ndb, dseq, _ = x_sample.shape
    depth = w_in.shape[0]
    past = cache_diff_k.shape[2]
    assert seq % CHUNK == 0 and dseq == CHUNK and d == GLA_DV

    bf = lambda w: w.astype(BF16)
    wg1, wu1, wd1 = bf(w_ffn1_gate), bf(w_ffn1_up), bf(w_ffn1_down)
    wg2, wu2, wd2 = bf(w_ffn2_gate), bf(w_ffn2_up), bf(w_ffn2_down)
    a0 = P_QD
    qd0 = a0 + GLA_RANK
    vd0 = qd0 + 2 * DIFF_QK
    zg0 = vd0 + DIFF_V
    w_main = jnp.concatenate([w_in[:, :, :a0], w_in[:, :, qd0:vd0], w_in[:, :, zg0:], w_in[:, :, vd0:zg0]],
                             axis=-1).astype(BF16)
    w_a = jnp.pad(w_in[:, :, a0:a0 + GLA_RANK], ((0, 0), (0, 0), (0, LANES - GLA_RANK))).astype(BF16)
    w_2 = jnp.pad(w_alpha2, ((0, 0), (0, LANES - GLA_RANK), (0, 0))).astype(BF16)
    b_a = b_alpha.reshape(depth, 1, GLA_DK)
    wgo, wdo, wo = bf(w_gla_o), bf(w_diff_o), bf(w_out)
    n_groups = DIFF_QK // DIFF_QK_HEAD
    lane_group = (jnp.arange(2 * LANES) % LANES) // DIFF_QK_HEAD
    group_ones = (lane_group[:, None] == lane_group[None, :LANES]).astype(BF16)
    qn = jnp.tile(q_norm, (1, n_groups)).reshape(depth, 1, DIFF_QK)
    kn = jnp.tile(k_norm, (1, n_groups)).reshape(depth, 1, DIFF_QK)
    dn = diff_norm.reshape(depth, DIFF_HEADS, 1, DIFF_V_HEAD)

    streams = []
    for x0, n_seqs, slen, start in ((x_prompt, nb, seq, 0), (x_sample, ndb, dseq, past)):
        rows = n_seqs * slen
        tm = _pick_tile(rows, ROW_TILES)
        period = max(slen, tm)
        assert period % slen == 0 and period % tm == 0
        tables = _rope_tables(start + jnp.arange(period) % slen)
        streams.append(dict(x=x0.reshape(rows, d), slen=slen, tables=tables, k=None, v=None, s=None))

    for l in range(depth):
        lambda_init = 0.8 - 0.6 * math.exp(-0.3 * l)
        li = jnp.full((1,), lambda_init, F32)
        for si, st in enumerate(streams):
            slen = st["slen"]
            x, h = _ffn(st["x"], norm_ffn1[l][None], wg1, wu1, wd1, l, gain_next=norm_mix[l][None])
            p = _proj(h, w_main, l, 0, P_COLS, PROJ_TN)
            st["v"] = _proj(h, w_main, l, W_VD, DIFF_V, DIFF_V, stacked=(depth, st["v"]))
            qd = _qk_rope(p, P_QD, qn[l], group_ones, st["tables"], BF16)
            st["k"] = _qk_rope(p, P_KD, kn[l], group_ones, st["tables"], F32, layer=l, stacked=(depth, st["k"]))
            ag, st["s"] = _gla(p, h, w_a, w_2, b_a, gla_norm, l, seq_len=slen, states_prev=st["s"],
                               depth=depth, s0=state_gla if si == 1 else None)
            if si == 0:
                ad = _attn_prompt(qd, st["k"], st["v"], li, diff_lambda, dn, l, seq_len=slen)
            else:
                ad = _attn_sample(qd, st["k"], st["v"], cache_diff_k, cache_diff_v, li, diff_lambda, dn, l,
                                  seq_len=slen)
            x = _mix_out(x, ag, ad, p, wgo, wdo, wo, l)
            st["x"] = _ffn(x, norm_ffn2[l][None], wg2, wu2, wd2, l)

    pr, sa = streams
    head_k = (DIFF_HEADS, 2 * DIFF_QK_HEAD)
    head_v = (DIFF_HEADS, DIFF_V_HEAD)
    return (pr["x"].reshape(nb, seq, d), sa["x"].reshape(ndb, dseq, d),
            pr["k"].reshape(depth, nb, seq, *head_k), pr["v"].reshape(depth, nb, seq, *head_v), pr["s"],
            sa["k"].reshape(depth, ndb, dseq, *head_k), sa["v"].reshape(depth, ndb, dseq, *head_v), sa["s"])
```
